```python
import jax, jax.numpy as jnp
from jax import lax
import numpy as np

D_MODEL = 1024
BATCH = 8
SEQ = 2048
DEPTH = 1
DEC_BATCH = 16
DEC_SEQ = 4096
PAST_LEN = 128

D_CONV_A = D_MODEL
K_SHORT = 3
D_CONV_B = D_MODEL
K_CONF = 31
N_EXPERTS = 256
TOP_K = 8
N_GROUPS = 8
TOPK_GROUPS = 4
D_EXPERT = 256
D_SHARED = 256
ROUTE_SCALE = 2.5
EXPERT_BLOCK = 128
LN_EPS = 1e-5
ALPHA = (2.0 * DEPTH) ** 0.25
BETA = (8.0 * DEPTH) ** -0.25
IN_PARTS = (D_CONV_A, D_CONV_A, D_CONV_A, D_CONV_B, D_CONV_B, D_MODEL, D_MODEL)
IN_COLS = int(sum(IN_PARTS))
IN_SPLITS = tuple(int(v) for v in np.cumsum(IN_PARTS)[:-1])

kernel_name = "hybrid_conv_moe_encoder"


def layer_norm(x, g, b):
    xf = x.astype(jnp.float32)
    mu = jnp.mean(xf, axis=-1, keepdims=True)
    var = jnp.mean(jnp.square(xf - mu), axis=-1, keepdims=True)
    y = (xf - mu) * lax.rsqrt(var + LN_EPS) * g.astype(jnp.float32) + b.astype(jnp.float32)
    return y.astype(x.dtype)


def depthwise_conv(x, w, b):
    k = w.shape[0]
    y = lax.conv_general_dilated(
        x, w[:, None, :].astype(x.dtype), window_strides=(1,),
        padding=[(k // 2, k // 2)], dimension_numbers=("NWC", "WIO", "NWC"),
        feature_group_count=x.shape[-1])
    return y + b.astype(x.dtype)


def mixer_sublayer(x, w_in, conv_a_w, conv_a_b, w_out_a, conv_b_w, conv_b_b,
                   ln_c_g, ln_c_b, w_out_b, w_o):
    u = jnp.einsum("bsd,dn->bsn", x, w_in)
    h_a, b_a, c_a, v_b, g_b, gate_a, gate_b = jnp.split(u, IN_SPLITS, axis=-1)
    y_a = jnp.einsum("bsc,cd->bsd", b_a * depthwise_conv(c_a * h_a, conv_a_w, conv_a_b), w_out_a)
    z = depthwise_conv(v_b * jax.nn.sigmoid(g_b), conv_b_w, conv_b_b)
    z = jax.nn.silu(layer_norm(z, ln_c_g, ln_c_b))
    y_b = jnp.einsum("bsc,cd->bsd", z, w_out_b)
    merged = jax.nn.sigmoid(gate_a) * y_a + jax.nn.sigmoid(gate_b) * y_b
    return jnp.einsum("bsd,de->bse", merged, w_o)


def route(xf, w_router, router_bias):
    t = xf.shape[0]
    scores = jax.nn.sigmoid(jnp.einsum("td,de->te", xf, w_router).astype(jnp.float32))
    biased = scores + router_bias.astype(jnp.float32)
    grp = biased.reshape(t, N_GROUPS, N_EXPERTS // N_GROUPS)
    grp_score = jnp.sum(lax.top_k(grp, 2)[0], axis=-1)
    _, top_grp = lax.top_k(grp_score, TOPK_GROUPS)
    grp_mask = jnp.any(top_grp[..., None] == jnp.arange(N_GROUPS)[None, None, :], axis=1)
    expert_mask = jnp.repeat(grp_mask, N_EXPERTS // N_GROUPS, axis=-1)
    _, idx = lax.top_k(jnp.where(expert_mask, biased, -jnp.inf), TOP_K)
    w = jnp.take_along_axis(scores, idx, axis=-1)
    w = w / jnp.sum(w, axis=-1, keepdims=True) * ROUTE_SCALE
    return idx, w.astype(xf.dtype)


def routed_experts(xf, idx, gate, w_gate, w_up, w_down):
    t, d = xf.shape
    n_assign = t * TOP_K
    flat_e = idx.reshape(-1).astype(jnp.int32)
    flat_tok = jnp.repeat(jnp.arange(t, dtype=jnp.int32), TOP_K)
    flat_w = gate.reshape(-1)
    order = jnp.argsort(flat_e)
    se, st, sw = flat_e[order], flat_tok[order], flat_w[order]
    counts = jnp.bincount(flat_e, length=N_EXPERTS).astype(jnp.int32)
    pcounts = (counts + EXPERT_BLOCK - 1) // EXPERT_BLOCK * EXPERT_BLOCK
    start = jnp.cumsum(counts) - counts
    pend = jnp.cumsum(pcounts)
    pstart = pend - pcounts
    dest = pstart[se] + (jnp.arange(n_assign, dtype=jnp.int32) - start[se])
    n_blocks = -(-(n_assign + N_EXPERTS * (EXPERT_BLOCK - 1)) // EXPERT_BLOCK)
    n_rows = n_blocks * EXPERT_BLOCK
    row_tok = jnp.full((n_rows,), t, jnp.int32).at[dest].set(st)
    row_w = jnp.zeros((n_rows,), xf.dtype).at[dest].set(sw)
    blk_start = jnp.arange(n_blocks, dtype=jnp.int32) * EXPERT_BLOCK
    blk_e = jnp.minimum(jnp.searchsorted(pend, blk_start, side="right"), N_EXPERTS - 1).astype(jnp.int32)
    xpad = jnp.concatenate([xf, jnp.zeros((1, d), xf.dtype)], axis=0)

    def step(acc, blk):
        tok, wt, e = blk
        r = xpad[tok]
        hdn = jax.nn.silu(r @ w_gate[e]) * (r @ w_up[e])
        out = (hdn @ w_down[e]) * wt[:, None]
        return acc.at[tok].add(out.astype(acc.dtype)), None

    acc, _ = lax.scan(step, jnp.zeros_like(xpad),
                      (row_tok.reshape(n_blocks, EXPERT_BLOCK),
                       row_w.reshape(n_blocks, EXPERT_BLOCK), blk_e))
    return acc[:t]


def moe_sublayer(xf, w_router, router_bias, w_gate, w_up, w_down, ws_gate, ws_up, ws_down):
    idx, gate = route(xf, w_router, router_bias)
    routed = routed_experts(xf, idx, gate, w_gate, w_up, w_down)
    shared = (jax.nn.silu(xf @ ws_gate) * (xf @ ws_up)) @ ws_down
    return routed + shared


def encoder_trunk(x, w_in, conv_a_w, conv_a_b, w_out_a, conv_b_w, conv_b_b, ln_c_g, ln_c_b,
                  w_out_b, w_o, ln1_g, ln1_b, w_router, router_bias, w_gate, w_up, w_down,
                  ws_gate, ws_up, ws_down, ln2_g, ln2_b):
    bsz, seq, d = x.shape
    for l in range(DEPTH):
        mix = mixer_sublayer(x, w_in[l], conv_a_w[l], conv_a_b[l], w_out_a[l], conv_b_w[l],
                             conv_b_b[l], ln_c_g[l], ln_c_b[l], w_out_b[l], w_o[l])
        x = layer_norm(ALPHA * x + mix, ln1_g[l], ln1_b[l])
        xf = x.reshape(bsz * seq, d)
        ffn = moe_sublayer(xf, w_router[l], router_bias[l], w_gate[l], w_up[l], w_down[l],
                           ws_gate[l], ws_up[l], ws_down[l])
        x = layer_norm(ALPHA * xf + ffn, ln2_g[l], ln2_b[l]).reshape(bsz, seq, d)
    return x


def setup_inputs(seed: int = 0) -> dict:
    key = jax.random.key(seed)
    ks = jax.random.split(key, 32)
    f32 = jnp.float32

    def nrm(k, shape, scale):
        return jax.random.normal(k, shape, f32) * scale

    L, D, E, F, FS = DEPTH, D_MODEL, N_EXPERTS, D_EXPERT, D_SHARED
    return {
        "x_prompt": nrm(ks[0], (BATCH, SEQ, D), 1.0),
        "x_sample": nrm(ks[1], (DEC_BATCH, DEC_SEQ, D), 1.0),
        "w_in": nrm(ks[2], (L, D, IN_COLS), D ** -0.5),
        "conv_a_w": nrm(ks[3], (L, K_SHORT, D_CONV_A), K_SHORT ** -0.5),
        "conv_a_b": nrm(ks[4], (L, D_CONV_A), 0.01),
        "w_out_a": nrm(ks[5], (L, D_CONV_A, D), D_CONV_A ** -0.5),
        "conv_b_w": nrm(ks[6], (L, K_CONF, D_CONV_B), K_CONF ** -0.5),
        "conv_b_b": nrm(ks[7], (L, D_CONV_B), 0.01),
        "ln_c_g": 1.0 + nrm(ks[8], (L, D_CONV_B), 0.01),
        "ln_c_b": nrm(ks[9], (L, D_CONV_B), 0.01),
        "w_out_b": nrm(ks[10], (L, D_CONV_B, D), D_CONV_B ** -0.5),
        "w_o": nrm(ks[11], (L, D, D), D ** -0.5 * BETA),
        "ln1_g": 1.0 + nrm(ks[12], (L, D), 0.01),
        "ln1_b": nrm(ks[13], (L, D), 0.01),
        "w_router": nrm(ks[14], (L, D, E), D ** -0.5),
        "router_bias": nrm(ks[15], (L, E), 0.01),
        "w_gate": nrm(ks[16], (L, E, D, F), D ** -0.5),
        "w_up": nrm(ks[17], (L, E, D, F), D ** -0.5),
        "w_down": nrm(ks[18], (L, E, F, D), F ** -0.5 * BETA),
        "ws_gate": nrm(ks[19], (L, D, FS), D ** -0.5),
        "ws_up": nrm(ks[20], (L, D, FS), D ** -0.5),
        "ws_down": nrm(ks[21], (L, FS, D), FS ** -0.5 * BETA),
        "ln2_g": 1.0 + nrm(ks[22], (L, D), 0.01),
        "ln2_b": nrm(ks[23], (L, D), 0.01),
    }


def reference(x_prompt, x_sample, w_in, conv_a_w, conv_a_b, w_out_a, conv_b_w, conv_b_b,
              ln_c_g, ln_c_b, w_out_b, w_o, ln1_g, ln1_b, w_router, router_bias, w_gate,
              w_up, w_down, ws_gate, ws_up, ws_down, ln2_g, ln2_b):
    y_prompt = encoder_trunk(x_prompt, w_in, conv_a_w, conv_a_b, w_out_a, conv_b_w, conv_b_b,
                             ln_c_g, ln_c_b, w_out_b, w_o, ln1_g, ln1_b, w_router, router_bias,
                             w_gate, w_up, w_down, ws_gate, ws_up, ws_down, ln2_g, ln2_b)
    y_sample = encoder_trunk(x_sample, w_in, conv_a_w, conv_a_b, w_out_a, conv_b_w, conv_b_b,
                             ln_c_g, ln_c_b, w_out_b, w_o, ln1_g, ln1_b, w_router, router_bias,
                             w_gate, w_up, w_down, ws_gate, ws_up, ws_down, ln2_g, ln2_b)
    return (y_prompt, y_sample)
```

```python
import functools

import jax
import jax.numpy as jnp
from jax import lax
from jax.experimental import pallas as pl
from jax.experimental.pallas import tpu as pltpu

K_SHORT = 3
K_CONF = 31
N_EXPERTS = 256
TOP_K = 8
N_GROUPS = 8
TOPK_GROUPS = 4
GROUP_SIZE = N_EXPERTS // N_GROUPS
ROUTE_SCALE = 2.5
LN_EPS = 1e-5
DEPTH = 1
ALPHA = (2.0 * DEPTH) ** 0.25
N_IN_PARTS = 7

V7X_LANES = 128
V7X_VMEM_LIMIT_BYTES = 56 * 1024 * 1024

HALO = 16
SEQ_TILE = 512
COL_CHUNK = 256
EXPERT_ROWS = 256
TOKEN_TILE = 256

_F32 = jnp.float32
_BF16 = jnp.bfloat16


def _dot(a, b):
    return jnp.dot(a, b, preferred_element_type=_F32)


def _dot_nt(a, b):
    return lax.dot_general(a, b, (((1,), (1,)), ((), ())), preferred_element_type=_F32)


def _sigmoid(v):
    return 1.0 / (1.0 + jnp.exp(-v))


def _layer_norm_rows(v, g, b):
    mu = jnp.mean(v, axis=-1, keepdims=True)
    d = v - mu
    var = jnp.mean(d * d, axis=-1, keepdims=True)
    return d * lax.rsqrt(var + LN_EPS) * g + b


def _route(logits_t, rbias, ts):
    scores = _sigmoid(logits_t)
    biased = scores + rbias
    neg = -jnp.inf
    iota_g = lax.broadcasted_iota(jnp.int32, (GROUP_SIZE, ts), 0)
    gscore = []
    for g in range(N_GROUPS):
        v = biased[g * GROUP_SIZE:(g + 1) * GROUP_SIZE]
        m1 = jnp.max(v, axis=0, keepdims=True)
        i1 = jnp.min(jnp.where(v == m1, iota_g, GROUP_SIZE), axis=0, keepdims=True)
        m2 = jnp.max(jnp.where(iota_g == i1, neg, v), axis=0, keepdims=True)
        gscore.append(m1 + m2)
    masked = []
    for g in range(N_GROUPS):
        ahead = jnp.zeros((1, ts), jnp.int32)
        for h in range(N_GROUPS):
            if h == g:
                continue
            before = gscore[h] > gscore[g]
            if h < g:
                before = before | (gscore[h] == gscore[g])
            ahead = ahead + before.astype(jnp.int32)
        keep = ahead < TOPK_GROUPS
        masked.append(jnp.where(keep, biased[g * GROUP_SIZE:(g + 1) * GROUP_SIZE], neg))
    v = jnp.concatenate(masked, axis=0)

    iota_e = lax.broadcasted_iota(jnp.int32, (N_EXPERTS, ts), 0)
    ids, wts = [], []
    onehot = jnp.zeros((N_EXPERTS, ts), _F32)
    for _ in range(TOP_K):
        m = jnp.max(v, axis=0, keepdims=True)
        idx = jnp.min(jnp.where(v == m, iota_e, N_EXPERTS), axis=0, keepdims=True)
        hit = iota_e == idx
        wts.append(jnp.sum(jnp.where(hit, scores, 0.0), axis=0, keepdims=True))
        ids.append(idx)
        v = jnp.where(hit, neg, v)
        onehot = jnp.where(hit, 1.0, onehot)
    total = wts[0]
    for w in wts[1:]:
        total = total + w
    wts = [w / total * ROUTE_SCALE for w in wts]
    return ids, wts, onehot


def _mixer_kernel(xm_ref, xp_ref, xn_ref,
                  wh_ref, wb_ref, wc_ref, wv_ref, wg_ref, wga_ref, wgb_ref,
                  caw_ref, cab_ref, cbw_ref, cbb_ref, lncg_ref, lncb_ref,
                  woa_ref, wob_ref, wo_ref, ln1g_ref, ln1b_ref,
                  wrh_ref, wrl_ref, rbias_ref, tri_ref,
                  x1_ref, eid_ref, wts_ref, rank_ref, cnt_ref,
                  xe_s, pa_s, z_s, ga_s, gb_s, ch_s, zin_s, base_s,
                  *, ts, cw, nc, tiles_per_seq):
    i = pl.program_id(0)
    c = pl.program_id(1)
    pos = lax.rem(i, tiles_per_seq)

    @pl.when((i == 0) & (c == 0))
    def _():
        base_s[...] = jnp.zeros_like(base_s)

    @pl.when(c == 0)
    def _():
        keep_prev = (pos > 0).astype(_F32)
        keep_next = (pos < tiles_per_seq - 1).astype(_F32)
        xe_s[0:HALO, :] = (xp_ref[...] * keep_prev).astype(_BF16)
        xe_s[HALO:HALO + ts, :] = xm_ref[...].astype(_BF16)
        xe_s[HALO + ts:, :] = (xn_ref[...] * keep_next).astype(_BF16)

    xe = xe_s[...]
    xc = xe_s[HALO:HALO + ts, :]

    ch_s[...] = _dot(xe, wh_ref[...]) * _dot(xe, wc_ref[...])
    conv_a = cab_ref[...]
    for j in range(K_SHORT):
        conv_a = conv_a + caw_ref[j:j + 1, :] * ch_s[pl.ds(HALO - K_SHORT // 2 + j, ts), :]
    pa_s[c] = (_dot(xc, wb_ref[...]) * conv_a).astype(_BF16)

    zin_s[...] = _dot(xe, wv_ref[...]) * _sigmoid(_dot(xe, wg_ref[...]))
    conv_b = cbb_ref[...]
    for j in range(K_CONF):
        conv_b = conv_b + cbw_ref[j:j + 1, :] * zin_s[pl.ds(HALO - K_CONF // 2 + j, ts), :]
    z_s[c] = conv_b

    ga_s[c] = _sigmoid(_dot(xc, wga_ref[...]))
    gb_s[c] = _sigmoid(_dot(xc, wgb_ref[...]))

    @pl.when(c == nc - 1)
    def _():
        d_model = nc * cw
        mu = sum(jnp.sum(z_s[k], axis=-1, keepdims=True) for k in range(nc)) / d_model
        var = sum(jnp.sum((z_s[k] - mu) ** 2, axis=-1, keepdims=True) for k in range(nc)) / d_model
        inv = lax.rsqrt(var + LN_EPS)
        y_a = jnp.zeros((ts, d_model), _F32)
        y_b = jnp.zeros((ts, d_model), _F32)
        for k in range(nc):
            cols = slice(k * cw, (k + 1) * cw)
            zn = (z_s[k] - mu) * inv * lncg_ref[:, cols] + lncb_ref[:, cols]
            sw = zn * _sigmoid(zn)
            y_b = y_b + _dot(sw.astype(_BF16), wob_ref[cols, :])
            y_a = y_a + _dot(pa_s[k], woa_ref[cols, :])
        mix = jnp.zeros((ts, d_model), _F32)
        for k in range(nc):
            cols = slice(k * cw, (k + 1) * cw)
            merged = ga_s[k] * y_a[:, cols] + gb_s[k] * y_b[:, cols]
            mix = mix + _dot(merged.astype(_BF16), wo_ref[cols, :])
        x1 = _layer_norm_rows(ALPHA * xm_ref[...] + mix, ln1g_ref[...], ln1b_ref[...])
        x1_ref[...] = x1

        xh = x1.astype(_BF16)
        xl = (x1 - xh.astype(_F32)).astype(_BF16)
        logits_t = _dot_nt(wrh_ref[...], xh) + (_dot_nt(wrl_ref[...], xh) + _dot_nt(wrh_ref[...], xl))
        ids, wts, onehot = _route(logits_t, rbias_ref[...], ts)

        before = _dot(onehot.astype(_BF16), tri_ref[...]) + base_s[:, 0:1]
        iota_e = lax.broadcasted_iota(jnp.int32, (N_EXPERTS, ts), 0)
        for k in range(TOP_K):
            eid_ref[k:k + 1, :] = ids[k]
            wts_ref[k:k + 1, :] = wts[k]
            r = jnp.sum(jnp.where(iota_e == ids[k], before, 0.0), axis=0, keepdims=True)
            rank_ref[k:k + 1, :] = r.astype(jnp.int32)
        base_s[...] = base_s[...] + jnp.sum(onehot, axis=1, keepdims=True)
        cnt_ref[...] = base_s[...].astype(jnp.int32)


def _mixer_call(x2d, seq_len, w_in_bf, conv_a_w, conv_a_b, conv_b_w, conv_b_b, ln_c_g, ln_c_b,
                w_out_a_bf, w_out_b_bf, w_o_bf, ln1_g, ln1_b, wr_hi, wr_lo, rbias, tri):
    t, d = x2d.shape
    ts, cw = SEQ_TILE, COL_CHUNK
    nc = d // cw
    assert seq_len % ts == 0 and d % cw == 0 and ts % HALO == 0
    n_tiles = t // ts
    halo_blocks_per_tile = ts // HALO
    n_halo_blocks = t // HALO

    def part_spec(p):
        return pl.BlockSpec((d, cw), lambda i, c, p=p: (0, p * nc + c))

    def chunk_rows(rows):
        return pl.BlockSpec((rows, cw), lambda i, c: (0, c))

    def const(shape):
        return pl.BlockSpec(shape, lambda i, c: (0,) * len(shape))

    tok_rows = lambda rows: pl.BlockSpec((rows, ts), lambda i, c: (0, i))

    in_specs = [
        pl.BlockSpec((ts, d), lambda i, c: (i, 0)),
        pl.BlockSpec((HALO, d), lambda i, c: (jnp.maximum(i * halo_blocks_per_tile - 1, 0), 0)),
        pl.BlockSpec((HALO, d),
                     lambda i, c: (jnp.minimum((i + 1) * halo_blocks_per_tile, n_halo_blocks - 1), 0)),
    ] + [part_spec(p) for p in range(N_IN_PARTS)] + [
        chunk_rows(K_SHORT), chunk_rows(1), chunk_rows(K_CONF), chunk_rows(1),
        const((1, d)), const((1, d)),
        const((d, d)), const((d, d)), const((d, d)),
        const((1, d)), const((1, d)),
        const((N_EXPERTS, d)), const((N_EXPERTS, d)), const((N_EXPERTS, 1)),
        const((ts, ts)),
    ]
    out_shape = [
        jax.ShapeDtypeStruct((t, d), _F32),
        jax.ShapeDtypeStruct((TOP_K, t), jnp.int32),
        jax.ShapeDtypeStruct((TOP_K, t), _F32),
        jax.ShapeDtypeStruct((TOP_K, t), jnp.int32),
        jax.ShapeDtypeStruct((N_EXPERTS, V7X_LANES), jnp.int32),
    ]
    out_specs = [
        pl.BlockSpec((ts, d), lambda i, c: (i, 0)),
        tok_rows(TOP_K), tok_rows(TOP_K), tok_rows(TOP_K),
        const((N_EXPERTS, V7X_LANES)),
    ]
    scratch = [
        pltpu.VMEM((ts + 2 * HALO, d), _BF16),
        pltpu.VMEM((nc, ts, cw), _BF16),
        pltpu.VMEM((nc, ts, cw), _F32),
        pltpu.VMEM((nc, ts, cw), _F32),
        pltpu.VMEM((nc, ts, cw), _F32),
        pltpu.VMEM((ts + 2 * HALO, cw), _F32),
        pltpu.VMEM((ts + 2 * HALO, cw), _F32),
        pltpu.VMEM((N_EXPERTS, V7X_LANES), _F32),
    ]
    kern = functools.partial(_mixer_kernel, ts=ts, cw=cw, nc=nc, tiles_per_seq=seq_len // ts)
    w_parts = [w_in_bf] * N_IN_PARTS
    return pl.pallas_call(
        kern,
        grid=(n_tiles, nc),
        in_specs=in_specs,
        out_specs=out_specs,
        out_shape=out_shape,
        scratch_shapes=scratch,
        compiler_params=pltpu.CompilerParams(
            dimension_semantics=("arbitrary", "arbitrary"),
            vmem_limit_bytes=V7X_VMEM_LIMIT_BYTES),
        name="mixer_router",
    )(x2d, x2d, x2d, *w_parts, conv_a_w, conv_a_b, conv_b_w, conv_b_b, ln_c_g, ln_c_b,
      w_out_a_bf, w_out_b_bf, w_o_bf, ln1_g, ln1_b, wr_hi, wr_lo, rbias, tri)


def _dispatch_kernel(slot_ref, x_hbm, xs_in_hbm, xs_hbm, sem, *, tt):
    del xs_in_hbm
    base = pl.program_id(0) * tt

    def row_copy(t, k):
        return pltpu.make_async_copy(
            x_hbm.at[pl.ds(base + t, 1)], xs_hbm.at[pl.ds(slot_ref[k, t], 1)], sem)

    def start(t, carry):
        for k in range(TOP_K):
            row_copy(t, k).start()
        return carry

    def wait(t, carry):
        for k in range(TOP_K):
            row_copy(t, k).wait()
        return carry

    lax.fori_loop(0, tt, start, 0)
    lax.fori_loop(0, tt, wait, 0)


def _dispatch_call(slot, x1, n_rows):
    t, d = x1.shape
    tt = TOKEN_TILE
    xs0 = jnp.zeros((n_rows, d), x1.dtype)
    return pl.pallas_call(
        functools.partial(_dispatch_kernel, tt=tt),
        grid=(t // tt,),
        in_specs=[
            pl.BlockSpec((TOP_K, tt), lambda i: (0, i), memory_space=pltpu.SMEM),
            pl.BlockSpec(memory_space=pl.ANY),
            pl.BlockSpec(memory_space=pl.ANY),
        ],
        out_specs=pl.BlockSpec(memory_space=pl.ANY),
        out_shape=jax.ShapeDtypeStruct((n_rows, d), x1.dtype),
        scratch_shapes=[pltpu.SemaphoreType.DMA],
        input_output_aliases={2: 0},
        compiler_params=pltpu.CompilerParams(dimension_semantics=("arbitrary",)),
        name="dispatch_rows",
    )(slot, x1, xs0)


def _experts_kernel(blk_e_ref, n_used_ref, xs_ref, wg_ref, wu_ref, wd_ref, ys_ref):
    del blk_e_ref

    @pl.when(pl.program_id(0) < n_used_ref[0])
    def _():
        x = xs_ref[...].astype(_BF16)
        g = _dot(x, wg_ref[0].astype(_BF16))
        u = _dot(x, wu_ref[0].astype(_BF16))
        h = (g * _sigmoid(g) * u).astype(_BF16)
        ys_ref[...] = _dot(h, wd_ref[0].astype(_BF16))


def _experts_call(blk_e, n_used, xs, w_gate, w_up, w_down):
    n_rows, d = xs.shape
    bm = EXPERT_ROWS
    f = w_gate.shape[-1]

    def row_block(b, blk_e_ref, n_used_ref):
        return (jnp.minimum(b, n_used_ref[0] - 1), 0)

    def expert_block(b, blk_e_ref, n_used_ref):
        return (blk_e_ref[b], 0, 0)

    grid_spec = pltpu.PrefetchScalarGridSpec(
        num_scalar_prefetch=2,
        grid=(n_rows // bm,),
        in_specs=[
            pl.BlockSpec((bm, d), row_block),
            pl.BlockSpec((1, d, f), expert_block),
            pl.BlockSpec((1, d, f), expert_block),
            pl.BlockSpec((1, f, d), expert_block),
        ],
        out_specs=pl.BlockSpec((bm, d), row_block),
    )
    return pl.pallas_call(
        _experts_kernel,
        grid_spec=grid_spec,
        out_shape=jax.ShapeDtypeStruct((n_rows, d), _F32),
        compiler_params=pltpu.CompilerParams(
            dimension_semantics=("arbitrary",),
            vmem_limit_bytes=V7X_VMEM_LIMIT_BYTES),
        name="expert_ffn",
    )(blk_e, n_used, xs, w_gate, w_up, w_down)


def _combine_kernel(slot_ref, wts_ref, x1_ref, ys_hbm, wsg_ref, wsu_ref, wsd_ref, g_ref, b_ref,
                    out_ref, gbuf, sem, *, tt):
    def row_copy(t, k):
        return pltpu.make_async_copy(
            ys_hbm.at[pl.ds(slot_ref[k, t], 1)], gbuf.at[k, pl.ds(t, 1)], sem)

    def start(t, carry):
        for k in range(TOP_K):
            row_copy(t, k).start()
        return carry

    def wait(t, carry):
        for k in range(TOP_K):
            row_copy(t, k).wait()
        return carry

    lax.fori_loop(0, tt, start, 0)

    x1 = x1_ref[...]
    xb = x1.astype(_BF16)
    g = _dot(xb, wsg_ref[...])
    u = _dot(xb, wsu_ref[...])
    acc = ALPHA * x1 + _dot((g * _sigmoid(g) * u).astype(_BF16), wsd_ref[...])

    lax.fori_loop(0, tt, wait, 0)
    w = wts_ref[...]
    for k in range(TOP_K):
        acc = acc + gbuf[k] * w[:, k:k + 1]
    out_ref[...] = _layer_norm_rows(acc, g_ref[...], b_ref[...])


def _combine_call(slot, wts_tk, x1, ys, ws_gate_bf, ws_up_bf, ws_down_bf, ln2_g, ln2_b):
    t, d = x1.shape
    tt = TOKEN_TILE
    fs = ws_gate_bf.shape[-1]

    def const(shape):
        return pl.BlockSpec(shape, lambda i: (0,) * len(shape))

    return pl.pallas_call(
        functools.partial(_combine_kernel, tt=tt),
        grid=(t // tt,),
        in_specs=[
            pl.BlockSpec((TOP_K, tt), lambda i: (0, i), memory_space=pltpu.SMEM),
            pl.BlockSpec((tt, TOP_K), lambda i: (i, 0)),
            pl.BlockSpec((tt, d), lambda i: (i, 0)),
            pl.BlockSpec(memory_space=pl.ANY),
            const((d, fs)), const((d, fs)), const((fs, d)),
            const((1, d)), const((1, d)),
        ],
        out_specs=pl.BlockSpec((tt, d), lambda i: (i, 0)),
        out_shape=jax.ShapeDtypeStruct((t, d), _F32),
        scratch_shapes=[pltpu.VMEM((TOP_K, tt, d), _F32), pltpu.SemaphoreType.DMA],
        compiler_params=pltpu.CompilerParams(
            dimension_semantics=("arbitrary",),
            vmem_limit_bytes=V7X_VMEM_LIMIT_BYTES),
        name="combine_shared_norm",
    )(slot, wts_tk, x1, ys, ws_gate_bf, ws_up_bf, ws_down_bf, ln2_g, ln2_b)


def _trunk(x, p):
    bsz, seq, d = x.shape
    t = bsz * seq
    x1, eid, wts, rank, cnt = _mixer_call(
        x.reshape(t, d), seq, p["w_in"], p["conv_a_w"], p["conv_a_b"], p["conv_b_w"], p["conv_b_b"],
        p["ln_c_g"], p["ln_c_b"], p["w_out_a"], p["w_out_b"], p["w_o"], p["ln1_g"], p["ln1_b"],
        p["wr_hi"], p["wr_lo"], p["rbias"], p["tri"])

    bm = EXPERT_ROWS
    counts = cnt[:, 0]
    pcounts = (counts + bm - 1) // bm * bm
    pend = jnp.cumsum(pcounts)
    pstart = pend - pcounts
    n_blocks = (t * TOP_K + N_EXPERTS * (bm - 1)) // bm
    n_used = jnp.maximum(pend[-1] // bm, 1).astype(jnp.int32)
    blk_start = jnp.minimum(jnp.arange(n_blocks, dtype=jnp.int32), n_used - 1) * bm
    blk_e = jnp.minimum(jnp.searchsorted(pend, blk_start, side="right"), N_EXPERTS - 1).astype(jnp.int32)
    slot = (pstart[eid] + rank).astype(jnp.int32)

    xs = _dispatch_call(slot, x1, n_blocks * bm)
    ys = _experts_call(blk_e, n_used.reshape(1), xs, p["w_gate"], p["w_up"], p["w_down"])
    out = _combine_call(slot, wts.T, x1, ys, p["ws_gate"], p["ws_up"], p["ws_down"],
                        p["ln2_g"], p["ln2_b"])
    return out.reshape(bsz, seq, d)


def _prepare_params(w_in, conv_a_w, conv_a_b, w_out_a, conv_b_w, conv_b_b, ln_c_g, ln_c_b,
                    w_out_b, w_o, ln1_g, ln1_b, w_router, router_bias, w_gate, w_up, w_down,
                    ws_gate, ws_up, ws_down, ln2_g, ln2_b):
    assert w_in.shape[0] == DEPTH == 1
    d = w_in.shape[1]
    wr = w_router[0].T
    wr_hi = wr.astype(_BF16)
    ts = SEQ_TILE
    return dict(
        w_in=w_in[0].astype(_BF16),
        conv_a_w=conv_a_w[0], conv_a_b=conv_a_b[0].reshape(1, d),
        conv_b_w=conv_b_w[0], conv_b_b=conv_b_b[0].reshape(1, d),
        ln_c_g=ln_c_g[0].reshape(1, d), ln_c_b=ln_c_b[0].reshape(1, d),
        w_out_a=w_out_a[0].astype(_BF16), w_out_b=w_out_b[0].astype(_BF16), w_o=w_o[0].astype(_BF16),
        ln1_g=ln1_g[0].reshape(1, d), ln1_b=ln1_b[0].reshape(1, d),
        wr_hi=wr_hi, wr_lo=(wr - wr_hi.astype(_F32)).astype(_BF16),
        rbias=router_bias[0].reshape(N_EXPERTS, 1),
        tri=(jnp.arange(ts)[:, None] < jnp.arange(ts)[None, :]).astype(_BF16),
        w_gate=w_gate[0], w_up=w_up[0], w_down=w_down[0],
        ws_gate=ws_gate[0].astype(_BF16), ws_up=ws_up[0].astype(_BF16), ws_down=ws_down[0].astype(_BF16),
        ln2_g=ln2_g[0].reshape(1, d), ln2_b=ln2_b[0].reshape(1, d),
    )


def kernel(x_prompt, x_sample, w_in, conv_a_w, conv_a_b, w_out_a, conv_b_w, conv_b_b, ln_c_g, ln_c_b,
           w_out_b, w_o, ln1_g, ln1_b, w_router, router_bias, w_gate, w_up, w_down, ws_gate, ws_up,
           ws_down, ln2_g, ln2_b):
    p = _prepare_params(w_in, conv_a_w, conv_a_b, w_out_a, conv_b_w, conv_b_b, ln_c_g, ln_c_b,
                        w_out_b, w_o, ln1_g, ln1_b, w_router, router_bias, w_gate, w_up, w_down,
                        ws_gate, ws_up, ws_down, ln2_g, ln2_b)
    return (_trunk(x_prompt, p), _trunk(x_sample, p))
```

```python
import functools

import jax
import jax.numpy as jnp
from jax import lax
from jax.experimental import pallas as pl
from jax.experimental.pallas import tpu as pltpu

K_SHORT = 3
K_CONF = 31
N_EXPERTS = 256
TOP_K = 8
N_GROUPS = 8
TOPK_GROUPS = 4
GROUP_SIZE = N_EXPERTS // N_GROUPS
ROUTE_SCALE = 2.5
LN_EPS = 1e-5
DEPTH = 1
ALPHA = (2.0 * DEPTH) ** 0.25
N_IN_PARTS = 7

V7X_LANES = 128
V7X_VMEM_LIMIT_BYTES = 56 * 1024 * 1024

HALO = 16
SEQ_TILE = 512
COL_CHUNK = 256
EXPERT_ROWS = 256
TOKEN_TILE = 256

_F32 = jnp.float32
_BF16 = jnp.bfloat16


def _dot(a, b):
    return jnp.dot(a, b, preferred_element_type=_F32)


def _dot_nt(a, b):
    return lax.dot_general(a, b, (((1,), (1,)), ((), ())), preferred_element_type=_F32)


def _sigmoid(v):
    return 1.0 / (1.0 + jnp.exp(-v))


def _layer_norm_rows(v, g, b):
    mu = jnp.mean(v, axis=-1, keepdims=True)
    d = v - mu
    var = jnp.mean(d * d, axis=-1, keepdims=True)
    return d * lax.rsqrt(var + LN_EPS) * g + b


def _route(logits_t, rbias, ts):
    scores = _sigmoid(logits_t)
    biased = scores + rbias
    neg = -jnp.inf
    iota_g = lax.broadcasted_iota(jnp.int32, (GROUP_SIZE, ts), 0)
    gscore = []
    for g in range(N_GROUPS):
        v = biased[g * GROUP_SIZE:(g + 1) * GROUP_SIZE]
        m1 = jnp.max(v, axis=0, keepdims=True)
        i1 = jnp.min(jnp.where(v == m1, iota_g, GROUP_SIZE), axis=0, keepdims=True)
        m2 = jnp.max(jnp.where(iota_g == i1, neg, v), axis=0, keepdims=True)
        gscore.append(m1 + m2)
    masked = []
    for g in range(N_GROUPS):
        ahead = jnp.zeros((1, ts), jnp.int32)
        for h in range(N_GROUPS):
            if h == g:
                continue
            before = gscore[h] > gscore[g]
            if h < g:
                before = before | (gscore[h] == gscore[g])
            ahead = ahead + before.astype(jnp.int32)
        keep = ahead < TOPK_GROUPS
        masked.append(jnp.where(keep, biased[g * GROUP_SIZE:(g + 1) * GROUP_SIZE], neg))
    v = jnp.concatenate(masked, axis=0)

    iota_e = lax.broadcasted_iota(jnp.int32, (N_EXPERTS, ts), 0)
    ids, wts = [], []
    onehot = jnp.zeros((N_EXPERTS, ts), _F32)
    for _ in range(TOP_K):
        m = jnp.max(v, axis=0, keepdims=True)
        idx = jnp.min(jnp.where(v == m, iota_e, N_EXPERTS), axis=0, keepdims=True)
        hit = iota_e == idx
        wts.append(jnp.sum(jnp.where(hit, scores, 0.0), axis=0, keepdims=True))
        ids.append(idx)
        v = jnp.where(hit, neg, v)
        onehot = jnp.where(hit, 1.0, onehot)
    total = wts[0]
    for w in wts[1:]:
        total = total + w
    wts = [w / total * ROUTE_SCALE for w in wts]
    return ids, wts, onehot


def _mixer_kernel(xm_ref, xp_ref, xn_ref,
                  wh_ref, wb_ref, wc_ref, wv_ref, wg_ref, wga_ref, wgb_ref,
                  caw_ref, cab_ref, cbw_ref, cbb_ref, lncg_ref, lncb_ref,
                  woa_ref, wob_ref, wo_ref, ln1g_ref, ln1b_ref,
                  wrh_ref, wrl_ref, rbias_ref, tri_ref,
                  x1_ref, eid_ref, wts_ref, rank_ref, cnt_ref,
                  xe_s, pa_s, z_s, ga_s, gb_s, ch_s, zin_s, base_s,
                  *, ts, cw, nc, tiles_per_seq):
    i = pl.program_id(0)
    c = pl.program_id(1)
    pos = lax.rem(i, tiles_per_seq)

    @pl.when((i == 0) & (c == 0))
    def _():
        base_s[...] = jnp.zeros_like(base_s)

    @pl.when(c == 0)
    def _():
        keep_prev = (pos > 0).astype(_F32)
        keep_next = (pos < tiles_per_seq - 1).astype(_F32)
        xe_s[0:HALO, :] = (xp_ref[...] * keep_prev).astype(_BF16)
        xe_s[HALO:HALO + ts, :] = xm_ref[...].astype(_BF16)
        xe_s[HALO + ts:, :] = (xn_ref[...] * keep_next).astype(_BF16)

    xe = xe_s[...]
    xc = xe_s[HALO:HALO + ts, :]

    ch_s[...] = _dot(xe, wh_ref[...]) * _dot(xe, wc_ref[...])
    conv_a = cab_ref[...]
    for j in range(K_SHORT):
        conv_a = conv_a + caw_ref[j:j + 1, :] * ch_s[pl.ds(HALO - K_SHORT // 2 + j, ts), :]
    pa_s[c] = (_dot(xc, wb_ref[...]) * conv_a).astype(_BF16)

    zin_s[...] = _dot(xe, wv_ref[...]) * _sigmoid(_dot(xe, wg_ref[...]))
    conv_b = cbb_ref[...]
    for j in range(K_CONF):
        conv_b = conv_b + cbw_ref[j:j + 1, :] * zin_s[pl.ds(HALO - K_CONF // 2 + j, ts), :]
    z_s[c] = conv_b

    ga_s[c] = _sigmoid(_dot(xc, wga_ref[...]))
    gb_s[c] = _sigmoid(_dot(xc, wgb_ref[...]))

    @pl.when(c == nc - 1)
    def _():
        d_model = nc * cw
        mu = sum(jnp.sum(z_s[k], axis=-1, keepdims=True) for k in range(nc)) / d_model
        var = sum(jnp.sum((z_s[k] - mu) ** 2, axis=-1, keepdims=True) for k in range(nc)) / d_model
        inv = lax.rsqrt(var + LN_EPS)
        y_a = jnp.zeros((ts, d_model), _F32)
        y_b = jnp.zeros((ts, d_model), _F32)
        for k in range(nc):
            cols = slice(k * cw, (k + 1) * cw)
            zn = (z_s[k] - mu) * inv * lncg_ref[:, cols] + lncb_ref[:, cols]
            sw = zn * _sigmoid(zn)
            y_b = y_b + _dot(sw.astype(_BF16), wob_ref[cols, :])
            y_a = y_a + _dot(pa_s[k], woa_ref[cols, :])
        mix = jnp.zeros((ts, d_model), _F32)
        for k in range(nc):
            cols = slice(k * cw, (k + 1) * cw)
            merged = ga_s[k] * y_a[:, cols] + gb_s[k] * y_b[:, cols]
            mix = mix + _dot(merged.astype(_BF16), wo_ref[cols, :])
        x1 = _layer_norm_rows(ALPHA * xm_ref[...] + mix, ln1g_ref[...], ln1b_ref[...])
        x1_ref[...] = x1

        xh = x1.astype(_BF16)
        xl = (x1 - xh.astype(_F32)).astype(_BF16)
        logits_t = _dot_nt(wrh_ref[...], xh) + (_dot_nt(wrl_ref[...], xh) + _dot_nt(wrh_ref[...], xl))
        ids, wts, onehot = _route(logits_t, rbias_ref[...], ts)

        before = _dot(onehot.astype(_BF16), tri_ref[...]) + base_s[:, 0:1]
        iota_e = lax.broadcasted_iota(jnp.int32, (N_EXPERTS, ts), 0)
        for k in range(TOP_K):
            eid_ref[k:k + 1, :] = ids[k]
            wts_ref[k:k + 1, :] = wts[k]
            r = jnp.sum(jnp.where(iota_e == ids[k], before, 0.0), axis=0, keepdims=True)
            rank_ref[k:k + 1, :] = r.astype(jnp.int32)
        base_s[...] = base_s[...] + jnp.sum(onehot, axis=1, keepdims=True)
        cnt_ref[...] = base_s[...].astype(jnp.int32)


def _mixer_call(x2d, seq_len, w_in_bf, conv_a_w, conv_a_b, conv_b_w, conv_b_b, ln_c_g, ln_c_b,
                w_out_a_bf, w_out_b_bf, w_o_bf, ln1_g, ln1_b, wr_hi, wr_lo, rbias, tri):
    t, d = x2d.shape
    ts, cw = SEQ_TILE, COL_CHUNK
    nc = d // cw
    assert seq_len % ts == 0 and d % cw == 0 and ts % HALO == 0
    n_tiles = t // ts
    halo_blocks_per_tile = ts // HALO
    n_halo_blocks = t // HALO

    def part_spec(p):
        return pl.BlockSpec((d, cw), lambda i, c, p=p: (0, p * nc + c))

    def chunk_rows(rows):
        return pl.BlockSpec((rows, cw), lambda i, c: (0, c))

    def const(shape):
        return pl.BlockSpec(shape, lambda i, c: (0,) * len(shape))

    tok_rows = lambda rows: pl.BlockSpec((rows, ts), lambda i, c: (0, i))

    in_specs = [
        pl.BlockSpec((ts, d), lambda i, c: (i, 0)),
        pl.BlockSpec((HALO, d), lambda i, c: (jnp.maximum(i * halo_blocks_per_tile - 1, 0), 0)),
        pl.BlockSpec((HALO, d),
                     lambda i, c: (jnp.minimum((i + 1) * halo_blocks_per_tile, n_halo_blocks - 1), 0)),
    ] + [part_spec(p) for p in range(N_IN_PARTS)] + [
        chunk_rows(K_SHORT), chunk_rows(1), chunk_rows(K_CONF), chunk_rows(1),
        const((1, d)), const((1, d)),
        const((d, d)), const((d, d)), const((d, d)),
        const((1, d)), const((1, d)),
        const((N_EXPERTS, d)), const((N_EXPERTS, d)), const((N_EXPERTS, 1)),
        const((ts, ts)),
    ]
    out_shape = [
        jax.ShapeDtypeStruct((t, d), _F32),
        jax.ShapeDtypeStruct((TOP_K, t), jnp.int32),
        jax.ShapeDtypeStruct((TOP_K, t), _F32),
        jax.ShapeDtypeStruct((TOP_K, t), jnp.int32),
        jax.ShapeDtypeStruct((N_EXPERTS, V7X_LANES), jnp.int32),
    ]
    out_specs = [
        pl.BlockSpec((ts, d), lambda i, c: (i, 0)),
        tok_rows(TOP_K), tok_rows(TOP_K), tok_rows(TOP_K),
        const((N_EXPERTS, V7X_LANES)),
    ]
    scratch = [
        pltpu.VMEM((ts + 2 * HALO, d), _BF16),
        pltpu.VMEM((nc, ts, cw), _BF16),
        pltpu.VMEM((nc, ts, cw), _F32),
        pltpu.VMEM((nc, ts, cw), _F32),
        pltpu.VMEM((nc, ts, cw), _F32),
        pltpu.VMEM((ts + 2 * HALO, cw), _F32),
        pltpu.VMEM((ts + 2 * HALO, cw), _F32),
        pltpu.VMEM((N_EXPERTS, V7X_LANES), _F32),
    ]
    kern = functools.partial(_mixer_kernel, ts=ts, cw=cw, nc=nc, tiles_per_seq=seq_len // ts)
    w_parts = [w_in_bf] * N_IN_PARTS
    return pl.pallas_call(
        kern,
        grid=(n_tiles, nc),
        in_specs=in_specs,
        out_specs=out_specs,
        out_shape=out_shape,
        scratch_shapes=scratch,
        compiler_params=pltpu.CompilerParams(
            dimension_semantics=("arbitrary", "arbitrary"),
            vmem_limit_bytes=V7X_VMEM_LIMIT_BYTES),
        name="mixer_router",
    )(x2d, x2d, x2d, *w_parts, conv_a_w, conv_a_b, conv_b_w, conv_b_b, ln_c_g, ln_c_b,
      w_out_a_bf, w_out_b_bf, w_o_bf, ln1_g, ln1_b, wr_hi, wr_lo, rbias, tri)


def _slot(pstart_ref, eid_ref, rank_ref, k, t):
    return pstart_ref[eid_ref[k, t]] + rank_ref[k, t]


def _dispatch_kernel(pstart_ref, cnt_ref, pcnt_ref, eid_ref, rank_ref, x_ref, xs_hbm, zero_s, sem, zsem,
                     *, tt):
    def row_copy(t, k):
        return pltpu.make_async_copy(
            x_ref.at[pl.ds(t, 1)],
            xs_hbm.at[pl.ds(_slot(pstart_ref, eid_ref, rank_ref, k, t), 1)], sem)

    def start(t, carry):
        for k in range(TOP_K):
            row_copy(t, k).start()
        return carry

    def wait(t, carry):
        for k in range(TOP_K):
            row_copy(t, k).wait()
        return carry

    lax.fori_loop(0, tt, start, 0)

    @pl.when(pl.program_id(0) == 0)
    def _():
        zero_s[...] = jnp.zeros_like(zero_s)

        def pad_copy(e, r):
            return pltpu.make_async_copy(
                zero_s.at[pl.ds(0, 1)], xs_hbm.at[pl.ds(pstart_ref[e] + r, 1)], zsem)

        def per_expert(fn):
            def body(e, carry):
                lax.fori_loop(cnt_ref[e], pcnt_ref[e], lambda r, c: (fn(e, r), c)[1], 0)
                return carry
            lax.fori_loop(0, N_EXPERTS, body, 0)

        per_expert(lambda e, r: pad_copy(e, r).start())
        per_expert(lambda e, r: pad_copy(e, r).wait())

    lax.fori_loop(0, tt, wait, 0)


def _dispatch_call(pstart, counts, pcounts, eid, rank, x1, n_rows):
    t, d = x1.shape
    tt = TOKEN_TILE
    smem_rows = pl.BlockSpec((TOP_K, tt), lambda i, *_: (0, i), memory_space=pltpu.SMEM)
    grid_spec = pltpu.PrefetchScalarGridSpec(
        num_scalar_prefetch=3,
        grid=(t // tt,),
        in_specs=[smem_rows, smem_rows, pl.BlockSpec((tt, d), lambda i, *_: (i, 0))],
        out_specs=pl.BlockSpec(memory_space=pl.ANY),
        scratch_shapes=[pltpu.VMEM((8, d), x1.dtype), pltpu.SemaphoreType.DMA, pltpu.SemaphoreType.DMA],
    )
    return pl.pallas_call(
        functools.partial(_dispatch_kernel, tt=tt),
        grid_spec=grid_spec,
        out_shape=jax.ShapeDtypeStruct((n_rows, d), x1.dtype),
        compiler_params=pltpu.CompilerParams(dimension_semantics=("arbitrary",)),
        name="dispatch_rows",
    )(pstart, counts, pcounts, eid, rank, x1)


def _experts_kernel(blk_e_ref, n_used_ref, xs_ref, wg_ref, wu_ref, wd_ref, ys_ref):
    del blk_e_ref

    @pl.when(pl.program_id(0) < n_used_ref[0])
    def _():
        x = xs_ref[...].astype(_BF16)
        g = _dot(x, wg_ref[0].astype(_BF16))
        u = _dot(x, wu_ref[0].astype(_BF16))
        h = (g * _sigmoid(g) * u).astype(_BF16)
        ys_ref[...] = _dot(h, wd_ref[0].astype(_BF16))


def _experts_call(blk_e, n_used, xs, w_gate, w_up, w_down):
    n_rows, d = xs.shape
    bm = EXPERT_ROWS
    f = w_gate.shape[-1]

    def row_block(b, blk_e_ref, n_used_ref):
        return (jnp.minimum(b, n_used_ref[0] - 1), 0)

    def expert_block(b, blk_e_ref, n_used_ref):
        return (blk_e_ref[b], 0, 0)

    grid_spec = pltpu.PrefetchScalarGridSpec(
        num_scalar_prefetch=2,
        grid=(n_rows // bm,),
        in_specs=[
            pl.BlockSpec((bm, d), row_block),
            pl.BlockSpec((1, d, f), expert_block),
            pl.BlockSpec((1, d, f), expert_block),
            pl.BlockSpec((1, f, d), expert_block),
        ],
        out_specs=pl.BlockSpec((bm, d), row_block),
    )
    return pl.pallas_call(
        _experts_kernel,
        grid_spec=grid_spec,
        out_shape=jax.ShapeDtypeStruct((n_rows, d), _F32),
        compiler_params=pltpu.CompilerParams(
            dimension_semantics=("arbitrary",),
            vmem_limit_bytes=V7X_VMEM_LIMIT_BYTES),
        name="expert_ffn",
    )(blk_e, n_used, xs, w_gate, w_up, w_down)


def _combine_kernel(pstart_ref, eid_ref, rank_ref, wts_ref, x1_ref, ys_hbm, wsg_ref, wsu_ref, wsd_ref,
                    g_ref, b_ref, out_ref, gbuf, sem, *, tt):
    def row_copy(t, k):
        return pltpu.make_async_copy(
            ys_hbm.at[pl.ds(_slot(pstart_ref, eid_ref, rank_ref, k, t), 1)],
            gbuf.at[k, pl.ds(t, 1)], sem)

    def start(t, carry):
        for k in range(TOP_K):
            row_copy(t, k).start()
        return carry

    def wait(t, carry):
        for k in range(TOP_K):
            row_copy(t, k).wait()
        return carry

    lax.fori_loop(0, tt, start, 0)

    x1 = x1_ref[...]
    xb = x1.astype(_BF16)
    g = _dot(xb, wsg_ref[...])
    u = _dot(xb, wsu_ref[...])
    acc = ALPHA * x1 + _dot((g * _sigmoid(g) * u).astype(_BF16), wsd_ref[...])

    lax.fori_loop(0, tt, wait, 0)
    w = wts_ref[...]
    for k in range(TOP_K):
        acc = acc + gbuf[k] * w[:, k:k + 1]
    out_ref[...] = _layer_norm_rows(acc, g_ref[...], b_ref[...])


def _combine_call(pstart, eid, rank, wts_tk, x1, ys, ws_gate_bf, ws_up_bf, ws_down_bf, ln2_g, ln2_b):
    t, d = x1.shape
    tt = TOKEN_TILE
    fs = ws_gate_bf.shape[-1]

    def const(shape):
        return pl.BlockSpec(shape, lambda i, *_: (0,) * len(shape))

    smem_rows = pl.BlockSpec((TOP_K, tt), lambda i, *_: (0, i), memory_space=pltpu.SMEM)
    grid_spec = pltpu.PrefetchScalarGridSpec(
        num_scalar_prefetch=1,
        grid=(t // tt,),
        in_specs=[
            smem_rows, smem_rows,
            pl.BlockSpec((tt, TOP_K), lambda i, *_: (i, 0)),
            pl.BlockSpec((tt, d), lambda i, *_: (i, 0)),
            pl.BlockSpec(memory_space=pl.ANY),
            const((d, fs)), const((d, fs)), const((fs, d)),
            const((1, d)), const((1, d)),
        ],
        out_specs=pl.BlockSpec((tt, d), lambda i, *_: (i, 0)),
        scratch_shapes=[pltpu.VMEM((TOP_K, tt, d), _F32), pltpu.SemaphoreType.DMA],
    )
    return pl.pallas_call(
        functools.partial(_combine_kernel, tt=tt),
        grid_spec=grid_spec,
        out_shape=jax.ShapeDtypeStruct((t, d), _F32),
        compiler_params=pltpu.CompilerParams(
            dimension_semantics=("arbitrary",),
            vmem_limit_bytes=V7X_VMEM_LIMIT_BYTES),
        name="combine_shared_norm",
    )(pstart, eid, rank, wts_tk, x1, ys, ws_gate_bf, ws_up_bf, ws_down_bf, ln2_g, ln2_b)


def _trunk(x, p):
    bsz, seq, d = x.shape
    t = bsz * seq
    x1, eid, wts, rank, cnt = _mixer_call(
        x.reshape(t, d), seq, p["w_in"], p["conv_a_w"], p["conv_a_b"], p["conv_b_w"], p["conv_b_b"],
        p["ln_c_g"], p["ln_c_b"], p["w_out_a"], p["w_out_b"], p["w_o"], p["ln1_g"], p["ln1_b"],
        p["wr_hi"], p["wr_lo"], p["rbias"], p["tri"])

    bm = EXPERT_ROWS
    counts = cnt[:, 0]
    pcounts = (counts + bm - 1) // bm * bm
    pend = jnp.cumsum(pcounts)
    pstart = pend - pcounts
    n_blocks = (t * TOP_K + N_EXPERTS * (bm - 1)) // bm
    n_used = jnp.maximum(pend[-1] // bm, 1).astype(jnp.int32)
    blk_start = jnp.minimum(jnp.arange(n_blocks, dtype=jnp.int32), n_used - 1) * bm
    blk_e = jnp.minimum(jnp.searchsorted(pend, blk_start, side="right"), N_EXPERTS - 1).astype(jnp.int32)
    pstart = pstart.astype(jnp.int32)

    xs = _dispatch_call(pstart, counts, pcounts, eid, rank, x1, n_blocks * bm)
    ys = _experts_call(blk_e, n_used.reshape(1), xs, p["w_gate"], p["w_up"], p["w_down"])
    out = _combine_call(pstart, eid, rank, wts.T, x1, ys, p["ws_gate"], p["ws_up"], p["ws_down"],
                        p["ln2_g"], p["ln2_b"])
    return out.reshape(bsz, seq, d)


def _prepare_params(w_in, conv_a_w, conv_a_b, w_out_a, conv_b_w, conv_b_b, ln_c_g, ln_c_b,
                    w_out_b, w_o, ln1_g, ln1_b, w_router, router_bias, w_gate, w_up, w_down,
                    ws_gate, ws_up, ws_down, ln2_g, ln2_b):
    assert w_in.shape[0] == DEPTH == 1
    d = w_in.shape[1]
    wr = w_router[0].T
    wr_hi = wr.astype(_BF16)
    ts = SEQ_TILE
    return dict(
        w_in=w_in[0].astype(_BF16),
        conv_a_w=conv_a_w[0], conv_a_b=conv_a_b[0].reshape(1, d),
        conv_b_w=conv_b_w[0], conv_b_b=conv_b_b[0].reshape(1, d),
        ln_c_g=ln_c_g[0].reshape(1, d), ln_c_b=ln_c_b[0].reshape(1, d),
        w_out_a=w_out_a[0].astype(_BF16), w_out_b=w_out_b[0].astype(_BF16), w_o=w_o[0].astype(_BF16),
        ln1_g=ln1_g[0].reshape(1, d), ln1_b=ln1_b[0].reshape(1, d),
        wr_hi=wr_hi, wr_lo=(wr - wr_hi.astype(_F32)).astype(_BF16),
        rbias=router_bias[0].reshape(N_EXPERTS, 1),
        tri=(jnp.arange(ts)[:, None] < jnp.arange(ts)[None, :]).astype(_BF16),
        w_gate=w_gate[0], w_up=w_up[0], w_down=w_down[0],
        ws_gate=ws_gate[0].astype(_BF16), ws_up=ws_up[0].astype(_BF16), ws_down=ws_down[0].astype(_BF16),
        ln2_g=ln2_g[0].reshape(1, d), ln2_b=ln2_b[0].reshape(1, d),
    )


def kernel(x_prompt, x_sample, w_in, conv_a_w, conv_a_b, w_out_a, conv_b_w, conv_b_b, ln_c_g, ln_c_b,
           w_out_b, w_o, ln1_g, ln1_b, w_router, router_bias, w_gate, w_up, w_down, ws_gate, ws_up,
           ws_down, ln2_g, ln2_b):
    p = _prepare_params(w_in, conv_a_w, conv_a_b, w_out_a, conv_b_w, conv_b_b, ln_c_g, ln_c_b,
                        w_out_b, w_o, ln1_g, ln1_b, w_router, router_bias, w_gate, w_up, w_down,
                        ws_gate, ws_up, ws_down, ln2_g, ln2_b)
    return (_trunk(x_prompt, p), _trunk(x_sample, p))
```

```python
import functools

import jax
import jax.numpy as jnp
from jax import lax
from jax.experimental import pallas as pl
from jax.experimental.pallas import tpu as pltpu
from jax.experimental.pallas import tpu_sc as plsc

K_SHORT = 3
K_CONF = 31
N_EXPERTS = 256
TOP_K = 8
N_GROUPS = 8
TOPK_GROUPS = 4
GROUP_SIZE = N_EXPERTS // N_GROUPS
ROUTE_SCALE = 2.5
LN_EPS = 1e-5
DEPTH = 1
ALPHA = (2.0 * DEPTH) ** 0.25
N_IN_PARTS = 7

V7X_LANES = 128
V7X_VMEM_LIMIT_BYTES = 56 * 1024 * 1024

HALO = 16
SEQ_TILE = 512
COL_CHUNK = 256
EXPERT_ROWS = 256
TOKEN_TILE = 256
SLOT_TILE = 2048

V7X_SC_CORES = 2
V7X_SC_SUBCORES = 16
SC_GATHER_ROWS = 32

_F32 = jnp.float32
_BF16 = jnp.bfloat16


def _dot(a, b):
    return jnp.dot(a, b, preferred_element_type=_F32)


def _dot_nt(a, b):
    return lax.dot_general(a, b, (((1,), (1,)), ((), ())), preferred_element_type=_F32)


def _sigmoid(v):
    return 1.0 / (1.0 + jnp.exp(-v))


def _layer_norm_rows(v, g, b):
    mu = jnp.mean(v, axis=-1, keepdims=True)
    d = v - mu
    var = jnp.mean(d * d, axis=-1, keepdims=True)
    return d * lax.rsqrt(var + LN_EPS) * g + b


def _route(logits_t, rbias, ts):
    scores = _sigmoid(logits_t)
    biased = scores + rbias
    neg = -jnp.inf
    iota_g = lax.broadcasted_iota(jnp.int32, (GROUP_SIZE, ts), 0)
    gscore = []
    for g in range(N_GROUPS):
        v = biased[g * GROUP_SIZE:(g + 1) * GROUP_SIZE]
        m1 = jnp.max(v, axis=0, keepdims=True)
        i1 = jnp.min(jnp.where(v == m1, iota_g, GROUP_SIZE), axis=0, keepdims=True)
        m2 = jnp.max(jnp.where(iota_g == i1, neg, v), axis=0, keepdims=True)
        gscore.append(m1 + m2)
    masked = []
    for g in range(N_GROUPS):
        ahead = jnp.zeros((1, ts), jnp.int32)
        for h in range(N_GROUPS):
            if h == g:
                continue
            before = gscore[h] > gscore[g]
            if h < g:
                before = before | (gscore[h] == gscore[g])
            ahead = ahead + before.astype(jnp.int32)
        keep = ahead < TOPK_GROUPS
        masked.append(jnp.where(keep, biased[g * GROUP_SIZE:(g + 1) * GROUP_SIZE], neg))
    v = jnp.concatenate(masked, axis=0)

    iota_e = lax.broadcasted_iota(jnp.int32, (N_EXPERTS, ts), 0)
    ids, wts = [], []
    onehot = jnp.zeros((N_EXPERTS, ts), _F32)
    for _ in range(TOP_K):
        m = jnp.max(v, axis=0, keepdims=True)
        idx = jnp.min(jnp.where(v == m, iota_e, N_EXPERTS), axis=0, keepdims=True)
        hit = iota_e == idx
        wts.append(jnp.sum(jnp.where(hit, scores, 0.0), axis=0, keepdims=True))
        ids.append(idx)
        v = jnp.where(hit, neg, v)
        onehot = jnp.where(hit, 1.0, onehot)
    total = wts[0]
    for w in wts[1:]:
        total = total + w
    wts = [w / total * ROUTE_SCALE for w in wts]
    return ids, wts, onehot


def _mixer_kernel(xm_ref, xp_ref, xn_ref,
                  wh_ref, wb_ref, wc_ref, wv_ref, wg_ref, wga_ref, wgb_ref,
                  caw_ref, cab_ref, cbw_ref, cbb_ref, lncg_ref, lncb_ref,
                  woa_ref, wob_ref, wo_ref, ln1g_ref, ln1b_ref,
                  wrh_ref, wrl_ref, rbias_ref, tri_ref,
                  x1_ref, eid_ref, wts_ref, rank_ref, cnt_ref,
                  xe_s, pa_s, z_s, ga_s, gb_s, ch_s, zin_s, base_s,
                  *, ts, cw, nc, tiles_per_seq):
    i = pl.program_id(0)
    c = pl.program_id(1)
    pos = lax.rem(i, tiles_per_seq)

    @pl.when((i == 0) & (c == 0))
    def _():
        base_s[...] = jnp.zeros_like(base_s)

    @pl.when(c == 0)
    def _():
        keep_prev = (pos > 0).astype(_F32)
        keep_next = (pos < tiles_per_seq - 1).astype(_F32)
        xe_s[0:HALO, :] = (xp_ref[...] * keep_prev).astype(_BF16)
        xe_s[HALO:HALO + ts, :] = xm_ref[...].astype(_BF16)
        xe_s[HALO + ts:, :] = (xn_ref[...] * keep_next).astype(_BF16)

    xe = xe_s[...]
    xc = xe_s[HALO:HALO + ts, :]

    ch_s[...] = _dot(xe, wh_ref[...]) * _dot(xe, wc_ref[...])
    conv_a = cab_ref[...]
    for j in range(K_SHORT):
        conv_a = conv_a + caw_ref[j:j + 1, :] * ch_s[pl.ds(HALO - K_SHORT // 2 + j, ts), :]
    pa_s[c] = (_dot(xc, wb_ref[...]) * conv_a).astype(_BF16)

    zin_s[...] = _dot(xe, wv_ref[...]) * _sigmoid(_dot(xe, wg_ref[...]))
    conv_b = cbb_ref[...]
    for j in range(K_CONF):
        conv_b = conv_b + cbw_ref[j:j + 1, :] * zin_s[pl.ds(HALO - K_CONF // 2 + j, ts), :]
    z_s[c] = conv_b

    ga_s[c] = _sigmoid(_dot(xc, wga_ref[...]))
    gb_s[c] = _sigmoid(_dot(xc, wgb_ref[...]))

    @pl.when(c == nc - 1)
    def _():
        d_model = nc * cw
        mu = sum(jnp.sum(z_s[k], axis=-1, keepdims=True) for k in range(nc)) / d_model
        var = sum(jnp.sum((z_s[k] - mu) ** 2, axis=-1, keepdims=True) for k in range(nc)) / d_model
        inv = lax.rsqrt(var + LN_EPS)
        y_a = jnp.zeros((ts, d_model), _F32)
        y_b = jnp.zeros((ts, d_model), _F32)
        for k in range(nc):
            cols = slice(k * cw, (k + 1) * cw)
            zn = (z_s[k] - mu) * inv * lncg_ref[:, cols] + lncb_ref[:, cols]
            sw = zn * _sigmoid(zn)
            y_b = y_b + _dot(sw.astype(_BF16), wob_ref[cols, :])
            y_a = y_a + _dot(pa_s[k], woa_ref[cols, :])
        mix = jnp.zeros((ts, d_model), _F32)
        for k in range(nc):
            cols = slice(k * cw, (k + 1) * cw)
            merged = ga_s[k] * y_a[:, cols] + gb_s[k] * y_b[:, cols]
            mix = mix + _dot(merged.astype(_BF16), wo_ref[cols, :])
        x1 = _layer_norm_rows(ALPHA * xm_ref[...] + mix, ln1g_ref[...], ln1b_ref[...])
        x1_ref[...] = x1

        xh = x1.astype(_BF16)
        xl = (x1 - xh.astype(_F32)).astype(_BF16)
        logits_t = _dot_nt(wrh_ref[...], xh) + (_dot_nt(wrl_ref[...], xh) + _dot_nt(wrh_ref[...], xl))
        ids, wts, onehot = _route(logits_t, rbias_ref[...], ts)

        before = _dot(onehot.astype(_BF16), tri_ref[...]) + base_s[:, 0:1]
        iota_e = lax.broadcasted_iota(jnp.int32, (N_EXPERTS, ts), 0)
        for k in range(TOP_K):
            eid_ref[k:k + 1, :] = ids[k]
            wts_ref[k:k + 1, :] = wts[k]
            r = jnp.sum(jnp.where(iota_e == ids[k], before, 0.0), axis=0, keepdims=True)
            rank_ref[k:k + 1, :] = r.astype(jnp.int32)
        base_s[...] = base_s[...] + jnp.sum(onehot, axis=1, keepdims=True)
        cnt_ref[...] = base_s[...].astype(jnp.int32)


def _mixer_call(x2d, seq_len, w_in_bf, conv_a_w, conv_a_b, conv_b_w, conv_b_b, ln_c_g, ln_c_b,
                w_out_a_bf, w_out_b_bf, w_o_bf, ln1_g, ln1_b, wr_hi, wr_lo, rbias, tri):
    t, d = x2d.shape
    ts, cw = SEQ_TILE, COL_CHUNK
    nc = d // cw
    assert seq_len % ts == 0 and d % cw == 0 and ts % HALO == 0
    n_tiles = t // ts
    halo_blocks_per_tile = ts // HALO
    n_halo_blocks = t // HALO

    def part_spec(p):
        return pl.BlockSpec((d, cw), lambda i, c, p=p: (0, p * nc + c))

    def chunk_rows(rows):
        return pl.BlockSpec((rows, cw), lambda i, c: (0, c))

    def const(shape):
        return pl.BlockSpec(shape, lambda i, c: (0,) * len(shape))

    tok_rows = lambda rows: pl.BlockSpec((rows, ts), lambda i, c: (0, i))

    in_specs = [
        pl.BlockSpec((ts, d), lambda i, c: (i, 0)),
        pl.BlockSpec((HALO, d), lambda i, c: (jnp.maximum(i * halo_blocks_per_tile - 1, 0), 0)),
        pl.BlockSpec((HALO, d),
                     lambda i, c: (jnp.minimum((i + 1) * halo_blocks_per_tile, n_halo_blocks - 1), 0)),
    ] + [part_spec(p) for p in range(N_IN_PARTS)] + [
        chunk_rows(K_SHORT), chunk_rows(1), chunk_rows(K_CONF), chunk_rows(1),
        const((1, d)), const((1, d)),
        const((d, d)), const((d, d)), const((d, d)),
        const((1, d)), const((1, d)),
        const((N_EXPERTS, d)), const((N_EXPERTS, d)), const((N_EXPERTS, 1)),
        const((ts, ts)),
    ]
    out_shape = [
        jax.ShapeDtypeStruct((t, d), _F32),
        jax.ShapeDtypeStruct((TOP_K, t), jnp.int32),
        jax.ShapeDtypeStruct((TOP_K, t), _F32),
        jax.ShapeDtypeStruct((TOP_K, t), jnp.int32),
        jax.ShapeDtypeStruct((N_EXPERTS, V7X_LANES), jnp.int32),
    ]
    out_specs = [
        pl.BlockSpec((ts, d), lambda i, c: (i, 0)),
        tok_rows(TOP_K), tok_rows(TOP_K), tok_rows(TOP_K),
        const((N_EXPERTS, V7X_LANES)),
    ]
    scratch = [
        pltpu.VMEM((ts + 2 * HALO, d), _BF16),
        pltpu.VMEM((nc, ts, cw), _BF16),
        pltpu.VMEM((nc, ts, cw), _F32),
        pltpu.VMEM((nc, ts, cw), _F32),
        pltpu.VMEM((nc, ts, cw), _F32),
        pltpu.VMEM((ts + 2 * HALO, cw), _F32),
        pltpu.VMEM((ts + 2 * HALO, cw), _F32),
        pltpu.VMEM((N_EXPERTS, V7X_LANES), _F32),
    ]
    kern = functools.partial(_mixer_kernel, ts=ts, cw=cw, nc=nc, tiles_per_seq=seq_len // ts)
    w_parts = [w_in_bf] * N_IN_PARTS
    return pl.pallas_call(
        kern,
        grid=(n_tiles, nc),
        in_specs=in_specs,
        out_specs=out_specs,
        out_shape=out_shape,
        scratch_shapes=scratch,
        compiler_params=pltpu.CompilerParams(
            dimension_semantics=("arbitrary", "arbitrary"),
            vmem_limit_bytes=V7X_VMEM_LIMIT_BYTES),
        name="mixer_router",
    )(x2d, x2d, x2d, *w_parts, conv_a_w, conv_a_b, conv_b_w, conv_b_b, ln_c_g, ln_c_b,
      w_out_a_bf, w_out_b_bf, w_o_bf, ln1_g, ln1_b, wr_hi, wr_lo, rbias, tri)


def _slot_kernel(pstart_ref, eid_ref, rank_ref, slot_ref):
    eid = eid_ref[...]

    def body(e, acc):
        return jnp.where(eid == e, pstart_ref[e], acc)

    slot_ref[...] = lax.fori_loop(0, N_EXPERTS, body, jnp.zeros_like(eid), unroll=8) + rank_ref[...]


def _slot_call(pstart, eid, rank):
    t = eid.shape[1]
    tl = min(SLOT_TILE, t)
    spec = pl.BlockSpec((TOP_K, tl), lambda i, *_: (0, i))
    return pl.pallas_call(
        _slot_kernel,
        grid_spec=pltpu.PrefetchScalarGridSpec(
            num_scalar_prefetch=1, grid=(t // tl,), in_specs=[spec, spec], out_specs=spec),
        out_shape=jax.ShapeDtypeStruct(eid.shape, jnp.int32),
        compiler_params=pltpu.CompilerParams(dimension_semantics=("arbitrary",)),
        name="slot_lookup",
    )(pstart, eid, rank)


def _sc_gather_rows(table, idx):
    n, = idx.shape
    d = table.shape[1]
    n_workers = V7X_SC_CORES * V7X_SC_SUBCORES
    ch = SC_GATHER_ROWS
    per_worker = n // n_workers
    assert n % (n_workers * ch) == 0
    mesh = plsc.VectorSubcoreMesh(core_axis_name="c", subcore_axis_name="s")

    @functools.partial(
        pl.kernel, mesh=mesh,
        out_type=jax.ShapeDtypeStruct((n, d), table.dtype),
        scratch_types=[pltpu.VMEM((ch,), jnp.int32), pltpu.VMEM((ch, d), table.dtype),
                       pltpu.SemaphoreType.DMA],
        name="sc_gather_rows",
    )
    def gather(table_hbm, idx_hbm, out_hbm, idx_v, rows_v, sem):
        worker = lax.axis_index("s") * V7X_SC_CORES + lax.axis_index("c")
        base = worker * per_worker

        @pl.loop(0, per_worker // ch)
        def _(c):
            off = base + c * ch
            pltpu.sync_copy(idx_hbm.at[pl.ds(off, ch)], idx_v)
            pltpu.async_copy(table_hbm.at[idx_v], rows_v, sem).wait()
            pltpu.sync_copy(rows_v, out_hbm.at[pl.ds(off, ch)])

    return gather(table, idx)


def _slot(pstart_ref, eid_ref, rank_ref, k, t):
    return pstart_ref[eid_ref[k, t]] + rank_ref[k, t]


def _dispatch_kernel(pstart_ref, cnt_ref, pcnt_ref, eid_ref, rank_ref, x_ref, xs_hbm, zero_s, sem, zsem,
                     *, tt):
    def row_copy(t, k):
        return pltpu.make_async_copy(
            x_ref.at[pl.ds(t, 1)],
            xs_hbm.at[pl.ds(_slot(pstart_ref, eid_ref, rank_ref, k, t), 1)], sem)

    def start(t, carry):
        for k in range(TOP_K):
            row_copy(t, k).start()
        return carry

    def wait(t, carry):
        for k in range(TOP_K):
            row_copy(t, k).wait()
        return carry

    lax.fori_loop(0, tt, start, 0)

    @pl.when(pl.program_id(0) == 0)
    def _():
        zero_s[...] = jnp.zeros_like(zero_s)

        def pad_copy(e, r):
            return pltpu.make_async_copy(
                zero_s.at[pl.ds(0, 1)], xs_hbm.at[pl.ds(pstart_ref[e] + r, 1)], zsem)

        def per_expert(fn):
            def body(e, carry):
                lax.fori_loop(cnt_ref[e], pcnt_ref[e], lambda r, c: (fn(e, r), c)[1], 0)
                return carry
            lax.fori_loop(0, N_EXPERTS, body, 0)

        per_expert(lambda e, r: pad_copy(e, r).start())
        per_expert(lambda e, r: pad_copy(e, r).wait())

    lax.fori_loop(0, tt, wait, 0)


def _dispatch_call(pstart, counts, pcounts, eid, rank, x1, n_rows):
    t, d = x1.shape
    tt = TOKEN_TILE
    smem_rows = pl.BlockSpec((TOP_K, tt), lambda i, *_: (0, i), memory_space=pltpu.SMEM)
    grid_spec = pltpu.PrefetchScalarGridSpec(
        num_scalar_prefetch=3,
        grid=(t // tt,),
        in_specs=[smem_rows, smem_rows, pl.BlockSpec((tt, d), lambda i, *_: (i, 0))],
        out_specs=pl.BlockSpec(memory_space=pl.ANY),
        scratch_shapes=[pltpu.VMEM((8, d), x1.dtype), pltpu.SemaphoreType.DMA, pltpu.SemaphoreType.DMA],
    )
    return pl.pallas_call(
        functools.partial(_dispatch_kernel, tt=tt),
        grid_spec=grid_spec,
        out_shape=jax.ShapeDtypeStruct((n_rows, d), x1.dtype),
        compiler_params=pltpu.CompilerParams(dimension_semantics=("arbitrary",)),
        name="dispatch_rows",
    )(pstart, counts, pcounts, eid, rank, x1)


def _experts_kernel(blk_e_ref, n_used_ref, xs_ref, wg_ref, wu_ref, wd_ref, ys_ref):
    del blk_e_ref

    @pl.when(pl.program_id(0) < n_used_ref[0])
    def _():
        x = xs_ref[...].astype(_BF16)
        g = _dot(x, wg_ref[0].astype(_BF16))
        u = _dot(x, wu_ref[0].astype(_BF16))
        h = (g * _sigmoid(g) * u).astype(_BF16)
        ys_ref[...] = _dot(h, wd_ref[0].astype(_BF16))


def _experts_call(blk_e, n_used, xs, w_gate, w_up, w_down):
    n_rows, d = xs.shape
    bm = EXPERT_ROWS
    f = w_gate.shape[-1]

    def row_block(b, blk_e_ref, n_used_ref):
        return (jnp.minimum(b, n_used_ref[0] - 1), 0)

    def expert_block(b, blk_e_ref, n_used_ref):
        return (blk_e_ref[b], 0, 0)

    grid_spec = pltpu.PrefetchScalarGridSpec(
        num_scalar_prefetch=2,
        grid=(n_rows // bm,),
        in_specs=[
            pl.BlockSpec((bm, d), row_block),
            pl.BlockSpec((1, d, f), expert_block),
            pl.BlockSpec((1, d, f), expert_block),
            pl.BlockSpec((1, f, d), expert_block),
        ],
        out_specs=pl.BlockSpec((bm, d), row_block),
    )
    return pl.pallas_call(
        _experts_kernel,
        grid_spec=grid_spec,
        out_shape=jax.ShapeDtypeStruct((n_rows, d), _F32),
        compiler_params=pltpu.CompilerParams(
            dimension_semantics=("arbitrary",),
            vmem_limit_bytes=V7X_VMEM_LIMIT_BYTES),
        name="expert_ffn",
    )(blk_e, n_used, xs, w_gate, w_up, w_down)


def _combine_kernel(wts_ref, x1_ref, rows_ref, wsg_ref, wsu_ref, wsd_ref, g_ref, b_ref, out_ref):
    x1 = x1_ref[...]
    xb = x1.astype(_BF16)
    g = _dot(xb, wsg_ref[...])
    u = _dot(xb, wsu_ref[...])
    acc = ALPHA * x1 + _dot((g * _sigmoid(g) * u).astype(_BF16), wsd_ref[...])
    w = wts_ref[...]
    for k in range(TOP_K):
        acc = acc + rows_ref[k] * w[:, k:k + 1]
    out_ref[...] = _layer_norm_rows(acc, g_ref[...], b_ref[...])


def _combine_call(wts_tk, x1, rows, ws_gate_bf, ws_up_bf, ws_down_bf, ln2_g, ln2_b):
    t, d = x1.shape
    tt = TOKEN_TILE
    fs = ws_gate_bf.shape[-1]

    def const(shape):
        return pl.BlockSpec(shape, lambda i: (0,) * len(shape))

    return pl.pallas_call(
        _combine_kernel,
        grid=(t // tt,),
        in_specs=[
            pl.BlockSpec((tt, TOP_K), lambda i: (i, 0)),
            pl.BlockSpec((tt, d), lambda i: (i, 0)),
            pl.BlockSpec((TOP_K, tt, d), lambda i: (0, i, 0)),
            const((d, fs)), const((d, fs)), const((fs, d)),
            const((1, d)), const((1, d)),
        ],
        out_specs=pl.BlockSpec((tt, d), lambda i: (i, 0)),
        out_shape=jax.ShapeDtypeStruct((t, d), _F32),
        compiler_params=pltpu.CompilerParams(
            dimension_semantics=("arbitrary",),
            vmem_limit_bytes=V7X_VMEM_LIMIT_BYTES),
        name="combine_shared_norm",
    )(wts_tk, x1, rows, ws_gate_bf, ws_up_bf, ws_down_bf, ln2_g, ln2_b)


def _trunk(x, p):
    bsz, seq, d = x.shape
    t = bsz * seq
    x1, eid, wts, rank, cnt = _mixer_call(
        x.reshape(t, d), seq, p["w_in"], p["conv_a_w"], p["conv_a_b"], p["conv_b_w"], p["conv_b_b"],
        p["ln_c_g"], p["ln_c_b"], p["w_out_a"], p["w_out_b"], p["w_o"], p["ln1_g"], p["ln1_b"],
        p["wr_hi"], p["wr_lo"], p["rbias"], p["tri"])

    bm = EXPERT_ROWS
    counts = cnt[:, 0]
    pcounts = (counts + bm - 1) // bm * bm
    pend = jnp.cumsum(pcounts)
    pstart = pend - pcounts
    n_blocks = (t * TOP_K + N_EXPERTS * (bm - 1)) // bm
    n_used = jnp.maximum(pend[-1] // bm, 1).astype(jnp.int32)
    blk_start = jnp.minimum(jnp.arange(n_blocks, dtype=jnp.int32), n_used - 1) * bm
    blk_e = jnp.minimum(jnp.searchsorted(pend, blk_start, side="right"), N_EXPERTS - 1).astype(jnp.int32)
    pstart = pstart.astype(jnp.int32)

    xs = _dispatch_call(pstart, counts, pcounts, eid, rank, x1, n_blocks * bm)
    ys = _experts_call(blk_e, n_used.reshape(1), xs, p["w_gate"], p["w_up"], p["w_down"])
    slot = _slot_call(pstart, eid, rank)
    rows = _sc_gather_rows(ys, slot.reshape(TOP_K * t)).reshape(TOP_K, t, d)
    out = _combine_call(wts.T, x1, rows, p["ws_gate"], p["ws_up"], p["ws_down"], p["ln2_g"], p["ln2_b"])
    return out.reshape(bsz, seq, d)


def _prepare_params(w_in, conv_a_w, conv_a_b, w_out_a, conv_b_w, conv_b_b, ln_c_g, ln_c_b,
                    w_out_b, w_o, ln1_g, ln1_b, w_router, router_bias, w_gate, w_up, w_down,
                    ws_gate, ws_up, ws_down, ln2_g, ln2_b):
    assert w_in.shape[0] == DEPTH == 1
    d = w_in.shape[1]
    wr = w_router[0].T
    wr_hi = wr.astype(_BF16)
    ts = SEQ_TILE
    return dict(
        w_in=w_in[0].astype(_BF16),
        conv_a_w=conv_a_w[0], conv_a_b=conv_a_b[0].reshape(1, d),
        conv_b_w=conv_b_w[0], conv_b_b=conv_b_b[0].reshape(1, d),
        ln_c_g=ln_c_g[0].reshape(1, d), ln_c_b=ln_c_b[0].reshape(1, d),
        w_out_a=w_out_a[0].astype(_BF16), w_out_b=w_out_b[0].astype(_BF16), w_o=w_o[0].astype(_BF16),
        ln1_g=ln1_g[0].reshape(1, d), ln1_b=ln1_b[0].reshape(1, d),
        wr_hi=wr_hi, wr_lo=(wr - wr_hi.astype(_F32)).astype(_BF16),
        rbias=router_bias[0].reshape(N_EXPERTS, 1),
        tri=(jnp.arange(ts)[:, None] < jnp.arange(ts)[None, :]).astype(_BF16),
        w_gate=w_gate[0], w_up=w_up[0], w_down=w_down[0],
        ws_gate=ws_gate[0].astype(_BF16), ws_up=ws_up[0].astype(_BF16), ws_down=ws_down[0].astype(_BF16),
        ln2_g=ln2_g[0].reshape(1, d), ln2_b=ln2_b[0].reshape(1, d),
    )


def kernel(x_prompt, x_sample, w_in, conv_a_w, conv_a_b, w_out_a, conv_b_w, conv_b_b, ln_c_g, ln_c_b,
           w_out_b, w_o, ln1_g, ln1_b, w_router, router_bias, w_gate, w_up, w_down, ws_gate, ws_up,
           ws_down, ln2_g, ln2_b):
    p = _prepare_params(w_in, conv_a_w, conv_a_b, w_out_a, conv_b_w, conv_b_b, ln_c_g, ln_c_b,
                        w_out_b, w_o, ln1_g, ln1_b, w_router, router_bias, w_gate, w_up, w_down,
                        ws_gate, ws_up, ws_down, ln2_g, ln2_b)
    return (_trunk(x_prompt, p), _trunk(x_sample, p))
```

```python
import functools

import jax
import jax.numpy as jnp
from jax import lax
from jax.experimental import pallas as pl
from jax.experimental.pallas import tpu as pltpu
from jax.experimental.pallas import tpu_sc as plsc

K_SHORT = 3
K_CONF = 31
N_EXPERTS = 256
TOP_K = 8
N_GROUPS = 8
TOPK_GROUPS = 4
GROUP_SIZE = N_EXPERTS // N_GROUPS
ROUTE_SCALE = 2.5
LN_EPS = 1e-5
DEPTH = 1
ALPHA = (2.0 * DEPTH) ** 0.25
N_IN_PARTS = 7

V7X_LANES = 128
V7X_VMEM_LIMIT_BYTES = 56 * 1024 * 1024

HALO = 16
SEQ_TILE = 512
COL_CHUNK = 256
EXPERT_ROWS = 256
TOKEN_TILE = 256
SLOT_TILE = 2048

V7X_SC_CORES = 2
V7X_SC_SUBCORES = 16
SC_GATHER_ROWS = 32

_F32 = jnp.float32
_BF16 = jnp.bfloat16


def _dot(a, b):
    return jnp.dot(a, b, preferred_element_type=_F32)


def _dot_nt(a, b):
    return lax.dot_general(a, b, (((1,), (1,)), ((), ())), preferred_element_type=_F32)


def _sigmoid(v):
    return 1.0 / (1.0 + jnp.exp(-v))


def _layer_norm_rows(v, g, b):
    mu = jnp.mean(v, axis=-1, keepdims=True)
    d = v - mu
    var = jnp.mean(d * d, axis=-1, keepdims=True)
    return d * lax.rsqrt(var + LN_EPS) * g + b


def _route(logits_t, rbias, ts):
    scores = _sigmoid(logits_t)
    biased = scores + rbias
    neg = -jnp.inf
    iota_g = lax.broadcasted_iota(jnp.int32, (GROUP_SIZE, ts), 0)
    gscore = []
    for g in range(N_GROUPS):
        v = biased[g * GROUP_SIZE:(g + 1) * GROUP_SIZE]
        m1 = jnp.max(v, axis=0, keepdims=True)
        i1 = jnp.min(jnp.where(v == m1, iota_g, GROUP_SIZE), axis=0, keepdims=True)
        m2 = jnp.max(jnp.where(iota_g == i1, neg, v), axis=0, keepdims=True)
        gscore.append(m1 + m2)
    masked = []
    for g in range(N_GROUPS):
        ahead = jnp.zeros((1, ts), jnp.int32)
        for h in range(N_GROUPS):
            if h == g:
                continue
            before = gscore[h] > gscore[g]
            if h < g:
                before = before | (gscore[h] == gscore[g])
            ahead = ahead + before.astype(jnp.int32)
        keep = ahead < TOPK_GROUPS
        masked.append(jnp.where(keep, biased[g * GROUP_SIZE:(g + 1) * GROUP_SIZE], neg))
    v = jnp.concatenate(masked, axis=0)

    iota_e = lax.broadcasted_iota(jnp.int32, (N_EXPERTS, ts), 0)
    ids, wts = [], []
    onehot = jnp.zeros((N_EXPERTS, ts), _F32)
    for _ in range(TOP_K):
        m = jnp.max(v, axis=0, keepdims=True)
        idx = jnp.min(jnp.where(v == m, iota_e, N_EXPERTS), axis=0, keepdims=True)
        hit = iota_e == idx
        wts.append(jnp.sum(jnp.where(hit, scores, 0.0), axis=0, keepdims=True))
        ids.append(idx)
        v = jnp.where(hit, neg, v)
        onehot = jnp.where(hit, 1.0, onehot)
    total = wts[0]
    for w in wts[1:]:
        total = total + w
    wts = [w / total * ROUTE_SCALE for w in wts]
    return ids, wts, onehot


def _mixer_kernel(xm_ref, xp_ref, xn_ref,
                  wh_ref, wb_ref, wc_ref, wv_ref, wg_ref, wga_ref, wgb_ref,
                  caw_ref, cab_ref, cbw_ref, cbb_ref, lncg_ref, lncb_ref,
                  woa_ref, wob_ref, wo_ref, ln1g_ref, ln1b_ref,
                  wrh_ref, wrl_ref, rbias_ref, tri_ref,
                  x1_ref, eid_ref, wts_ref, rank_ref, cnt_ref,
                  xe_s, pa_s, z_s, ga_s, gb_s, ch_s, zin_s, base_s,
                  *, ts, cw, nc, tiles_per_seq):
    i = pl.program_id(0)
    c = pl.program_id(1)
    pos = lax.rem(i, tiles_per_seq)

    @pl.when((i == 0) & (c == 0))
    def _():
        base_s[...] = jnp.zeros_like(base_s)

    @pl.when(c == 0)
    def _():
        keep_prev = (pos > 0).astype(_F32)
        keep_next = (pos < tiles_per_seq - 1).astype(_F32)
        xe_s[0:HALO, :] = (xp_ref[...] * keep_prev).astype(_BF16)
        xe_s[HALO:HALO + ts, :] = xm_ref[...].astype(_BF16)
        xe_s[HALO + ts:, :] = (xn_ref[...] * keep_next).astype(_BF16)

    xe = xe_s[...]
    xc = xe_s[HALO:HALO + ts, :]

    ch_s[...] = _dot(xe, wh_ref[...]) * _dot(xe, wc_ref[...])
    conv_a = cab_ref[...]
    for j in range(K_SHORT):
        conv_a = conv_a + caw_ref[j:j + 1, :] * ch_s[pl.ds(HALO - K_SHORT // 2 + j, ts), :]
    pa_s[c] = (_dot(xc, wb_ref[...]) * conv_a).astype(_BF16)

    zin_s[...] = _dot(xe, wv_ref[...]) * _sigmoid(_dot(xe, wg_ref[...]))
    conv_b = cbb_ref[...]
    for j in range(K_CONF):
        conv_b = conv_b + cbw_ref[j:j + 1, :] * zin_s[pl.ds(HALO - K_CONF // 2 + j, ts), :]
    z_s[c] = conv_b

    ga_s[c] = _sigmoid(_dot(xc, wga_ref[...]))
    gb_s[c] = _sigmoid(_dot(xc, wgb_ref[...]))

    @pl.when(c == nc - 1)
    def _():
        d_model = nc * cw
        mu = sum(jnp.sum(z_s[k], axis=-1, keepdims=True) for k in range(nc)) / d_model
        var = sum(jnp.sum((z_s[k] - mu) ** 2, axis=-1, keepdims=True) for k in range(nc)) / d_model
        inv = lax.rsqrt(var + LN_EPS)
        y_a = jnp.zeros((ts, d_model), _F32)
        y_b = jnp.zeros((ts, d_model), _F32)
        for k in range(nc):
            cols = slice(k * cw, (k + 1) * cw)
            zn = (z_s[k] - mu) * inv * lncg_ref[:, cols] + lncb_ref[:, cols]
            sw = zn * _sigmoid(zn)
            y_b = y_b + _dot(sw.astype(_BF16), wob_ref[cols, :])
            y_a = y_a + _dot(pa_s[k], woa_ref[cols, :])
        mix = jnp.zeros((ts, d_model), _F32)
        for k in range(nc):
            cols = slice(k * cw, (k + 1) * cw)
            merged = ga_s[k] * y_a[:, cols] + gb_s[k] * y_b[:, cols]
            mix = mix + _dot(merged.astype(_BF16), wo_ref[cols, :])
        x1 = _layer_norm_rows(ALPHA * xm_ref[...] + mix, ln1g_ref[...], ln1b_ref[...])
        x1_ref[...] = x1

        xh = x1.astype(_BF16)
        xl = (x1 - xh.astype(_F32)).astype(_BF16)
        logits_t = _dot_nt(wrh_ref[...], xh) + (_dot_nt(wrl_ref[...], xh) + _dot_nt(wrh_ref[...], xl))
        ids, wts, onehot = _route(logits_t, rbias_ref[...], ts)

        before = _dot(onehot.astype(_BF16), tri_ref[...]) + base_s[:, 0:1]
        iota_e = lax.broadcasted_iota(jnp.int32, (N_EXPERTS, ts), 0)
        for k in range(TOP_K):
            eid_ref[k:k + 1, :] = ids[k]
            wts_ref[k:k + 1, :] = wts[k]
            r = jnp.sum(jnp.where(iota_e == ids[k], before, 0.0), axis=0, keepdims=True)
            rank_ref[k:k + 1, :] = r.astype(jnp.int32)
        base_s[...] = base_s[...] + jnp.sum(onehot, axis=1, keepdims=True)
        cnt_ref[...] = base_s[...].astype(jnp.int32)


def _mixer_call(x2d, seq_len, w_in_bf, conv_a_w, conv_a_b, conv_b_w, conv_b_b, ln_c_g, ln_c_b,
                w_out_a_bf, w_out_b_bf, w_o_bf, ln1_g, ln1_b, wr_hi, wr_lo, rbias, tri):
    t, d = x2d.shape
    ts, cw = SEQ_TILE, COL_CHUNK
    nc = d // cw
    assert seq_len % ts == 0 and d % cw == 0 and ts % HALO == 0
    n_tiles = t // ts
    halo_blocks_per_tile = ts // HALO
    n_halo_blocks = t // HALO

    def part_spec(p):
        return pl.BlockSpec((d, cw), lambda i, c, p=p: (0, p * nc + c))

    def chunk_rows(rows):
        return pl.BlockSpec((rows, cw), lambda i, c: (0, c))

    def const(shape):
        return pl.BlockSpec(shape, lambda i, c: (0,) * len(shape))

    tok_rows = lambda rows: pl.BlockSpec((rows, ts), lambda i, c: (0, i))

    in_specs = [
        pl.BlockSpec((ts, d), lambda i, c: (i, 0)),
        pl.BlockSpec((HALO, d), lambda i, c: (jnp.maximum(i * halo_blocks_per_tile - 1, 0), 0)),
        pl.BlockSpec((HALO, d),
                     lambda i, c: (jnp.minimum((i + 1) * halo_blocks_per_tile, n_halo_blocks - 1), 0)),
    ] + [part_spec(p) for p in range(N_IN_PARTS)] + [
        chunk_rows(K_SHORT), chunk_rows(1), chunk_rows(K_CONF), chunk_rows(1),
        const((1, d)), const((1, d)),
        const((d, d)), const((d, d)), const((d, d)),
        const((1, d)), const((1, d)),
        const((N_EXPERTS, d)), const((N_EXPERTS, d)), const((N_EXPERTS, 1)),
        const((ts, ts)),
    ]
    out_shape = [
        jax.ShapeDtypeStruct((t, d), _F32),
        jax.ShapeDtypeStruct((TOP_K, t), jnp.int32),
        jax.ShapeDtypeStruct((TOP_K, t), _F32),
        jax.ShapeDtypeStruct((TOP_K, t), jnp.int32),
        jax.ShapeDtypeStruct((N_EXPERTS, V7X_LANES), jnp.int32),
    ]
    out_specs = [
        pl.BlockSpec((ts, d), lambda i, c: (i, 0)),
        tok_rows(TOP_K), tok_rows(TOP_K), tok_rows(TOP_K),
        const((N_EXPERTS, V7X_LANES)),
    ]
    scratch = [
        pltpu.VMEM((ts + 2 * HALO, d), _BF16),
        pltpu.VMEM((nc, ts, cw), _BF16),
        pltpu.VMEM((nc, ts, cw), _F32),
        pltpu.VMEM((nc, ts, cw), _F32),
        pltpu.VMEM((nc, ts, cw), _F32),
        pltpu.VMEM((ts + 2 * HALO, cw), _F32),
        pltpu.VMEM((ts + 2 * HALO, cw), _F32),
        pltpu.VMEM((N_EXPERTS, V7X_LANES), _F32),
    ]
    kern = functools.partial(_mixer_kernel, ts=ts, cw=cw, nc=nc, tiles_per_seq=seq_len // ts)
    w_parts = [w_in_bf] * N_IN_PARTS
    return pl.pallas_call(
        kern,
        grid=(n_tiles, nc),
        in_specs=in_specs,
        out_specs=out_specs,
        out_shape=out_shape,
        scratch_shapes=scratch,
        compiler_params=pltpu.CompilerParams(
            dimension_semantics=("arbitrary", "arbitrary"),
            vmem_limit_bytes=V7X_VMEM_LIMIT_BYTES),
        name="mixer_router",
    )(x2d, x2d, x2d, *w_parts, conv_a_w, conv_a_b, conv_b_w, conv_b_b, ln_c_g, ln_c_b,
      w_out_a_bf, w_out_b_bf, w_o_bf, ln1_g, ln1_b, wr_hi, wr_lo, rbias, tri)


def _slot_kernel(pstart_ref, eid_ref, rank_ref, slot_ref):
    eid = eid_ref[...]

    def body(e, acc):
        return jnp.where(eid == e, pstart_ref[e], acc)

    slot_ref[...] = lax.fori_loop(0, N_EXPERTS, body, jnp.zeros_like(eid), unroll=8) + rank_ref[...]


def _slot_call(pstart, eid, rank):
    t = eid.shape[1]
    tl = min(SLOT_TILE, t)
    spec = pl.BlockSpec((TOP_K, tl), lambda i, *_: (0, i))
    return pl.pallas_call(
        _slot_kernel,
        grid_spec=pltpu.PrefetchScalarGridSpec(
            num_scalar_prefetch=1, grid=(t // tl,), in_specs=[spec, spec], out_specs=spec),
        out_shape=jax.ShapeDtypeStruct(eid.shape, jnp.int32),
        compiler_params=pltpu.CompilerParams(dimension_semantics=("arbitrary",)),
        name="slot_lookup",
    )(pstart, eid, rank)


def _sc_gather_rows(table, idx):
    n, = idx.shape
    d = table.shape[1]
    n_workers = V7X_SC_CORES * V7X_SC_SUBCORES
    ch = SC_GATHER_ROWS
    per_worker = n // n_workers
    assert n % (n_workers * ch) == 0
    mesh = plsc.VectorSubcoreMesh(core_axis_name="c", subcore_axis_name="s")

    @functools.partial(
        pl.kernel, mesh=mesh,
        out_type=jax.ShapeDtypeStruct((n, d), table.dtype),
        scratch_types=[pltpu.VMEM((ch,), jnp.int32), pltpu.VMEM((ch, d), table.dtype),
                       pltpu.SemaphoreType.DMA],
        name="sc_gather_rows",
    )
    def gather(table_hbm, idx_hbm, out_hbm, idx_v, rows_v, sem):
        worker = lax.axis_index("s") * V7X_SC_CORES + lax.axis_index("c")
        base = worker * per_worker

        @pl.loop(0, per_worker // ch)
        def _(c):
            off = base + c * ch
            pltpu.sync_copy(idx_hbm.at[pl.ds(off, ch)], idx_v)
            pltpu.async_copy(table_hbm.at[idx_v], rows_v, sem).wait()
            pltpu.sync_copy(rows_v, out_hbm.at[pl.ds(off, ch)])

    return gather(table, idx)


def _sc_dispatch_rows(x, slot_flat, n_rows):
    t, d = x.shape
    n_workers = V7X_SC_CORES * V7X_SC_SUBCORES
    ch = SC_GATHER_ROWS
    per_worker = t // n_workers
    assert t % (n_workers * ch) == 0 and slot_flat.shape == (TOP_K * t,)
    mesh = plsc.VectorSubcoreMesh(core_axis_name="c", subcore_axis_name="s")

    @functools.partial(
        pl.kernel, mesh=mesh,
        out_type=jax.ShapeDtypeStruct((n_rows, d), x.dtype),
        scratch_types=[pltpu.VMEM((ch,), jnp.int32)] * TOP_K + [
            pltpu.VMEM((ch, d), x.dtype), pltpu.SemaphoreType.DMA, pltpu.SemaphoreType.DMA],
        name="sc_dispatch_rows",
    )
    def dispatch(x_hbm, slot_hbm, out_hbm, *scratch):
        idx_vs, rows_v, idx_sem, row_sem = scratch[:TOP_K], scratch[TOP_K], scratch[TOP_K + 1], scratch[TOP_K + 2]
        worker = lax.axis_index("s") * V7X_SC_CORES + lax.axis_index("c")
        base = worker * per_worker

        @pl.loop(0, per_worker // ch)
        def _(c):
            off = base + c * ch
            idx_copies = [pltpu.async_copy(slot_hbm.at[pl.ds(k * t + off, ch)], idx_vs[k], idx_sem)
                          for k in range(TOP_K)]
            pltpu.sync_copy(x_hbm.at[pl.ds(off, ch)], rows_v)
            for cp in idx_copies:
                cp.wait()
            row_copies = [pltpu.async_copy(rows_v, out_hbm.at[idx_vs[k]], row_sem) for k in range(TOP_K)]
            for cp in row_copies:
                cp.wait()

    return dispatch(x, slot_flat)


def _experts_kernel(blk_e_ref, blk_valid_ref, n_used_ref, xs_ref, wg_ref, wu_ref, wd_ref, ys_ref):
    del blk_e_ref
    b = pl.program_id(0)

    @pl.when(b < n_used_ref[0])
    def _():
        row = lax.broadcasted_iota(jnp.int32, xs_ref.shape, 0)
        x = jnp.where(row < blk_valid_ref[b], xs_ref[...], 0.0).astype(_BF16)
        g = _dot(x, wg_ref[0].astype(_BF16))
        u = _dot(x, wu_ref[0].astype(_BF16))
        h = (g * _sigmoid(g) * u).astype(_BF16)
        ys_ref[...] = _dot(h, wd_ref[0].astype(_BF16))


def _experts_call(blk_e, blk_valid, n_used, xs, w_gate, w_up, w_down):
    n_rows, d = xs.shape
    bm = EXPERT_ROWS
    f = w_gate.shape[-1]

    def row_block(b, blk_e_ref, blk_valid_ref, n_used_ref):
        return (jnp.minimum(b, n_used_ref[0] - 1), 0)

    def expert_block(b, blk_e_ref, blk_valid_ref, n_used_ref):
        return (blk_e_ref[b], 0, 0)

    grid_spec = pltpu.PrefetchScalarGridSpec(
        num_scalar_prefetch=3,
        grid=(n_rows // bm,),
        in_specs=[
            pl.BlockSpec((bm, d), row_block),
            pl.BlockSpec((1, d, f), expert_block),
            pl.BlockSpec((1, d, f), expert_block),
            pl.BlockSpec((1, f, d), expert_block),
        ],
        out_specs=pl.BlockSpec((bm, d), row_block),
    )
    return pl.pallas_call(
        _experts_kernel,
        grid_spec=grid_spec,
        out_shape=jax.ShapeDtypeStruct((n_rows, d), _F32),
        compiler_params=pltpu.CompilerParams(
            dimension_semantics=("arbitrary",),
            vmem_limit_bytes=V7X_VMEM_LIMIT_BYTES),
        name="expert_ffn",
    )(blk_e, blk_valid, n_used, xs, w_gate, w_up, w_down)


def _combine_kernel(wts_ref, x1_ref, rows_ref, wsg_ref, wsu_ref, wsd_ref, g_ref, b_ref, out_ref):
    x1 = x1_ref[...]
    xb = x1.astype(_BF16)
    g = _dot(xb, wsg_ref[...])
    u = _dot(xb, wsu_ref[...])
    acc = ALPHA * x1 + _dot((g * _sigmoid(g) * u).astype(_BF16), wsd_ref[...])
    w = wts_ref[...]
    for k in range(TOP_K):
        acc = acc + rows_ref[k] * w[:, k:k + 1]
    out_ref[...] = _layer_norm_rows(acc, g_ref[...], b_ref[...])


def _combine_call(wts_tk, x1, rows, ws_gate_bf, ws_up_bf, ws_down_bf, ln2_g, ln2_b):
    t, d = x1.shape
    tt = TOKEN_TILE
    fs = ws_gate_bf.shape[-1]

    def const(shape):
        return pl.BlockSpec(shape, lambda i: (0,) * len(shape))

    return pl.pallas_call(
        _combine_kernel,
        grid=(t // tt,),
        in_specs=[
            pl.BlockSpec((tt, TOP_K), lambda i: (i, 0)),
            pl.BlockSpec((tt, d), lambda i: (i, 0)),
            pl.BlockSpec((TOP_K, tt, d), lambda i: (0, i, 0)),
            const((d, fs)), const((d, fs)), const((fs, d)),
            const((1, d)), const((1, d)),
        ],
        out_specs=pl.BlockSpec((tt, d), lambda i: (i, 0)),
        out_shape=jax.ShapeDtypeStruct((t, d), _F32),
        compiler_params=pltpu.CompilerParams(
            dimension_semantics=("arbitrary",),
            vmem_limit_bytes=V7X_VMEM_LIMIT_BYTES),
        name="combine_shared_norm",
    )(wts_tk, x1, rows, ws_gate_bf, ws_up_bf, ws_down_bf, ln2_g, ln2_b)


def _trunk(x, p):
    bsz, seq, d = x.shape
    t = bsz * seq
    x1, eid, wts, rank, cnt = _mixer_call(
        x.reshape(t, d), seq, p["w_in"], p["conv_a_w"], p["conv_a_b"], p["conv_b_w"], p["conv_b_b"],
        p["ln_c_g"], p["ln_c_b"], p["w_out_a"], p["w_out_b"], p["w_o"], p["ln1_g"], p["ln1_b"],
        p["wr_hi"], p["wr_lo"], p["rbias"], p["tri"])

    bm = EXPERT_ROWS
    counts = cnt[:, 0]
    pcounts = (counts + bm - 1) // bm * bm
    pend = jnp.cumsum(pcounts)
    pstart = pend - pcounts
    n_blocks = (t * TOP_K + N_EXPERTS * (bm - 1)) // bm
    n_used = jnp.maximum(pend[-1] // bm, 1).astype(jnp.int32)
    blk_start = jnp.minimum(jnp.arange(n_blocks, dtype=jnp.int32), n_used - 1) * bm
    blk_e = jnp.minimum(jnp.searchsorted(pend, blk_start, side="right"), N_EXPERTS - 1).astype(jnp.int32)
    blk_valid = jnp.clip(counts[blk_e] - (blk_start - pstart[blk_e]), 0, bm).astype(jnp.int32)
    pstart = pstart.astype(jnp.int32)

    slot = _slot_call(pstart, eid, rank).reshape(TOP_K * t)
    xs = _sc_dispatch_rows(x1, slot, n_blocks * bm)
    ys = _experts_call(blk_e, blk_valid, n_used.reshape(1), xs, p["w_gate"], p["w_up"], p["w_down"])
    rows = _sc_gather_rows(ys, slot).reshape(TOP_K, t, d)
    out = _combine_call(wts.T, x1, rows, p["ws_gate"], p["ws_up"], p["ws_down"], p["ln2_g"], p["ln2_b"])
    return out.reshape(bsz, seq, d)


def _prepare_params(w_in, conv_a_w, conv_a_b, w_out_a, conv_b_w, conv_b_b, ln_c_g, ln_c_b,
                    w_out_b, w_o, ln1_g, ln1_b, w_router, router_bias, w_gate, w_up, w_down,
                    ws_gate, ws_up, ws_down, ln2_g, ln2_b):
    assert w_in.shape[0] == DEPTH == 1
    d = w_in.shape[1]
    wr = w_router[0].T
    wr_hi = wr.astype(_BF16)
    ts = SEQ_TILE
    return dict(
        w_in=w_in[0].astype(_BF16),
        conv_a_w=conv_a_w[0], conv_a_b=conv_a_b[0].reshape(1, d),
        conv_b_w=conv_b_w[0], conv_b_b=conv_b_b[0].reshape(1, d),
        ln_c_g=ln_c_g[0].reshape(1, d), ln_c_b=ln_c_b[0].reshape(1, d),
        w_out_a=w_out_a[0].astype(_BF16), w_out_b=w_out_b[0].astype(_BF16), w_o=w_o[0].astype(_BF16),
        ln1_g=ln1_g[0].reshape(1, d), ln1_b=ln1_b[0].reshape(1, d),
        wr_hi=wr_hi, wr_lo=(wr - wr_hi.astype(_F32)).astype(_BF16),
        rbias=router_bias[0].reshape(N_EXPERTS, 1),
        tri=(jnp.arange(ts)[:, None] < jnp.arange(ts)[None, :]).astype(_BF16),
        w_gate=w_gate[0], w_up=w_up[0], w_down=w_down[0],
        ws_gate=ws_gate[0].astype(_BF16), ws_up=ws_up[0].astype(_BF16), ws_down=ws_down[0].astype(_BF16),
        ln2_g=ln2_g[0].reshape(1, d), ln2_b=ln2_b[0].reshape(1, d),
    )


def kernel(x_prompt, x_sample, w_in, conv_a_w, conv_a_b, w_out_a, conv_b_w, conv_b_b, ln_c_g, ln_c_b,
           w_out_b, w_o, ln1_g, ln1_b, w_router, router_bias, w_gate, w_up, w_down, ws_gate, ws_up,
           ws_down, ln2_g, ln2_b):
    p = _prepare_params(w_in, conv_a_w, conv_a_b, w_out_a, conv_b_w, conv_b_b, ln_c_g, ln_c_b,
                        w_out_b, w_o, ln1_g, ln1_b, w_router, router_bias, w_gate, w_up, w_down,
                        ws_gate, ws_up, ws_down, ln2_g, ln2_b)
    return (_trunk(x_prompt, p), _trunk(x_sample, p))
```

```python
import functools

import jax
import jax.numpy as jnp
from jax import lax
from jax.experimental import pallas as pl
from jax.experimental.pallas import tpu as pltpu
from jax.experimental.pallas import tpu_sc as plsc

K_SHORT = 3
K_CONF = 31
N_EXPERTS = 256
TOP_K = 8
N_GROUPS = 8
TOPK_GROUPS = 4
GROUP_SIZE = N_EXPERTS // N_GROUPS
ROUTE_SCALE = 2.5
LN_EPS = 1e-5
DEPTH = 1
ALPHA = (2.0 * DEPTH) ** 0.25
N_IN_PARTS = 7

V7X_LANES = 128
V7X_VMEM_LIMIT_BYTES = 56 * 1024 * 1024

HALO = 16
SEQ_TILE = 512
COL_CHUNK = 256
EXPERT_ROWS = 256
TOKEN_TILE = 256
SLOT_TILE = 2048

V7X_SC_CORES = 2
V7X_SC_SUBCORES = 16
SC_CHUNK_ROWS = 64

_F32 = jnp.float32
_BF16 = jnp.bfloat16
_HIGH_HALF = -65536


def _dot(a, b):
    return jnp.dot(a, b, preferred_element_type=_F32)


def _dot_nt(a, b):
    return lax.dot_general(a, b, (((1,), (1,)), ((), ())), preferred_element_type=_F32)


def _sigmoid(v):
    return 1.0 / (1.0 + jnp.exp(-v))


def _pack_bf16_pairs(v):
    h = v.shape[1] // 2
    hi = lax.bitcast_convert_type(v[:, :h].astype(_BF16).astype(_F32), jnp.int32)
    lo = lax.bitcast_convert_type(v[:, h:].astype(_BF16).astype(_F32), jnp.int32)
    return (hi & _HIGH_HALF) | lax.shift_right_logical(lo, 16)


def _unpack_bf16_pairs(p):
    hi = lax.bitcast_convert_type(p & _HIGH_HALF, _F32)
    lo = lax.bitcast_convert_type(lax.shift_left(p, 16), _F32)
    return hi, lo


def _layer_norm_rows(v, g, b):
    mu = jnp.mean(v, axis=-1, keepdims=True)
    d = v - mu
    var = jnp.mean(d * d, axis=-1, keepdims=True)
    return d * lax.rsqrt(var + LN_EPS) * g + b


def _route(logits_t, rbias, ts):
    scores = _sigmoid(logits_t)
    biased = scores + rbias
    neg = -jnp.inf
    iota_g = lax.broadcasted_iota(jnp.int32, (GROUP_SIZE, ts), 0)
    gscore = []
    for g in range(N_GROUPS):
        v = biased[g * GROUP_SIZE:(g + 1) * GROUP_SIZE]
        m1 = jnp.max(v, axis=0, keepdims=True)
        i1 = jnp.min(jnp.where(v == m1, iota_g, GROUP_SIZE), axis=0, keepdims=True)
        m2 = jnp.max(jnp.where(iota_g == i1, neg, v), axis=0, keepdims=True)
        gscore.append(m1 + m2)
    masked = []
    for g in range(N_GROUPS):
        ahead = jnp.zeros((1, ts), jnp.int32)
        for h in range(N_GROUPS):
            if h == g:
                continue
            before = gscore[h] > gscore[g]
            if h < g:
                before = before | (gscore[h] == gscore[g])
            ahead = ahead + before.astype(jnp.int32)
        keep = ahead < TOPK_GROUPS
        masked.append(jnp.where(keep, biased[g * GROUP_SIZE:(g + 1) * GROUP_SIZE], neg))
    v = jnp.concatenate(masked, axis=0)

    iota_e = lax.broadcasted_iota(jnp.int32, (N_EXPERTS, ts), 0)
    ids, wts = [], []
    onehot = jnp.zeros((N_EXPERTS, ts), _F32)
    for _ in range(TOP_K):
        m = jnp.max(v, axis=0, keepdims=True)
        idx = jnp.min(jnp.where(v == m, iota_e, N_EXPERTS), axis=0, keepdims=True)
        hit = iota_e == idx
        wts.append(jnp.sum(jnp.where(hit, scores, 0.0), axis=0, keepdims=True))
        ids.append(idx)
        v = jnp.where(hit, neg, v)
        onehot = jnp.where(hit, 1.0, onehot)
    total = wts[0]
    for w in wts[1:]:
        total = total + w
    wts = [w / total * ROUTE_SCALE for w in wts]
    return ids, wts, onehot


def _mixer_kernel(xm_ref, xp_ref, xn_ref,
                  wh_ref, wb_ref, wc_ref, wv_ref, wg_ref, wga_ref, wgb_ref,
                  caw_ref, cab_ref, cbw_ref, cbb_ref, lncg_ref, lncb_ref,
                  woa_ref, wob_ref, wo_ref, ln1g_ref, ln1b_ref,
                  wrh_ref, wrl_ref, rbias_ref, tri_ref,
                  x1_ref, x1p_ref, eid_ref, wts_ref, rank_ref, cnt_ref,
                  xe_s, pa_s, z_s, ga_s, gb_s, ch_s, zin_s, base_s,
                  *, ts, cw, nc, tiles_per_seq):
    i = pl.program_id(0)
    c = pl.program_id(1)
    pos = lax.rem(i, tiles_per_seq)

    @pl.when((i == 0) & (c == 0))
    def _():
        base_s[...] = jnp.zeros_like(base_s)

    @pl.when(c == 0)
    def _():
        keep_prev = (pos > 0).astype(_F32)
        keep_next = (pos < tiles_per_seq - 1).astype(_F32)
        xe_s[0:HALO, :] = (xp_ref[...] * keep_prev).astype(_BF16)
        xe_s[HALO:HALO + ts, :] = xm_ref[...].astype(_BF16)
        xe_s[HALO + ts:, :] = (xn_ref[...] * keep_next).astype(_BF16)

    xe = xe_s[...]
    xc = xe_s[HALO:HALO + ts, :]

    ch_s[...] = _dot(xe, wh_ref[...]) * _dot(xe, wc_ref[...])
    conv_a = cab_ref[...]
    for j in range(K_SHORT):
        conv_a = conv_a + caw_ref[j:j + 1, :] * ch_s[pl.ds(HALO - K_SHORT // 2 + j, ts), :]
    pa_s[c] = (_dot(xc, wb_ref[...]) * conv_a).astype(_BF16)

    zin_s[...] = _dot(xe, wv_ref[...]) * _sigmoid(_dot(xe, wg_ref[...]))
    conv_b = cbb_ref[...]
    for j in range(K_CONF):
        conv_b = conv_b + cbw_ref[j:j + 1, :] * zin_s[pl.ds(HALO - K_CONF // 2 + j, ts), :]
    z_s[c] = conv_b

    ga_s[c] = _sigmoid(_dot(xc, wga_ref[...]))
    gb_s[c] = _sigmoid(_dot(xc, wgb_ref[...]))

    @pl.when(c == nc - 1)
    def _():
        d_model = nc * cw
        mu = sum(jnp.sum(z_s[k], axis=-1, keepdims=True) for k in range(nc)) / d_model
        var = sum(jnp.sum((z_s[k] - mu) ** 2, axis=-1, keepdims=True) for k in range(nc)) / d_model
        inv = lax.rsqrt(var + LN_EPS)
        y_a = jnp.zeros((ts, d_model), _F32)
        y_b = jnp.zeros((ts, d_model), _F32)
        for k in range(nc):
            cols = slice(k * cw, (k + 1) * cw)
            zn = (z_s[k] - mu) * inv * lncg_ref[:, cols] + lncb_ref[:, cols]
            sw = zn * _sigmoid(zn)
            y_b = y_b + _dot(sw.astype(_BF16), wob_ref[cols, :])
            y_a = y_a + _dot(pa_s[k], woa_ref[cols, :])
        mix = jnp.zeros((ts, d_model), _F32)
        for k in range(nc):
            cols = slice(k * cw, (k + 1) * cw)
            merged = ga_s[k] * y_a[:, cols] + gb_s[k] * y_b[:, cols]
            mix = mix + _dot(merged.astype(_BF16), wo_ref[cols, :])
        x1 = _layer_norm_rows(ALPHA * xm_ref[...] + mix, ln1g_ref[...], ln1b_ref[...])
        x1_ref[...] = x1
        x1p_ref[...] = _pack_bf16_pairs(x1)

        xh = x1.astype(_BF16)
        xl = (x1 - xh.astype(_F32)).astype(_BF16)
        logits_t = _dot_nt(wrh_ref[...], xh) + (_dot_nt(wrl_ref[...], xh) + _dot_nt(wrh_ref[...], xl))
        ids, wts, onehot = _route(logits_t, rbias_ref[...], ts)

        before = _dot(onehot.astype(_BF16), tri_ref[...]) + base_s[:, 0:1]
        iota_e = lax.broadcasted_iota(jnp.int32, (N_EXPERTS, ts), 0)
        for k in range(TOP_K):
            eid_ref[k:k + 1, :] = ids[k]
            wts_ref[k:k + 1, :] = wts[k]
            r = jnp.sum(jnp.where(iota_e == ids[k], before, 0.0), axis=0, keepdims=True)
            rank_ref[k:k + 1, :] = r.astype(jnp.int32)
        base_s[...] = base_s[...] + jnp.sum(onehot, axis=1, keepdims=True)
        cnt_ref[...] = base_s[...].astype(jnp.int32)


def _mixer_call(x2d, seq_len, w_in_bf, conv_a_w, conv_a_b, conv_b_w, conv_b_b, ln_c_g, ln_c_b,
                w_out_a_bf, w_out_b_bf, w_o_bf, ln1_g, ln1_b, wr_hi, wr_lo, rbias, tri):
    t, d = x2d.shape
    ts, cw = SEQ_TILE, COL_CHUNK
    nc = d // cw
    assert seq_len % ts == 0 and d % cw == 0 and ts % HALO == 0
    n_tiles = t // ts
    halo_blocks_per_tile = ts // HALO
    n_halo_blocks = t // HALO

    def part_spec(p):
        return pl.BlockSpec((d, cw), lambda i, c, p=p: (0, p * nc + c))

    def chunk_rows(rows):
        return pl.BlockSpec((rows, cw), lambda i, c: (0, c))

    def const(shape):
        return pl.BlockSpec(shape, lambda i, c: (0,) * len(shape))

    tok_rows = lambda rows: pl.BlockSpec((rows, ts), lambda i, c: (0, i))

    in_specs = [
        pl.BlockSpec((ts, d), lambda i, c: (i, 0)),
        pl.BlockSpec((HALO, d), lambda i, c: (jnp.maximum(i * halo_blocks_per_tile - 1, 0), 0)),
        pl.BlockSpec((HALO, d),
                     lambda i, c: (jnp.minimum((i + 1) * halo_blocks_per_tile, n_halo_blocks - 1), 0)),
    ] + [part_spec(p) for p in range(N_IN_PARTS)] + [
        chunk_rows(K_SHORT), chunk_rows(1), chunk_rows(K_CONF), chunk_rows(1),
        const((1, d)), const((1, d)),
        const((d, d)), const((d, d)), const((d, d)),
        const((1, d)), const((1, d)),
        const((N_EXPERTS, d)), const((N_EXPERTS, d)), const((N_EXPERTS, 1)),
        const((ts, ts)),
    ]
    out_shape = [
        jax.ShapeDtypeStruct((t, d), _F32),
        jax.ShapeDtypeStruct((t, d // 2), jnp.int32),
        jax.ShapeDtypeStruct((TOP_K, t), jnp.int32),
        jax.ShapeDtypeStruct((TOP_K, t), _F32),
        jax.ShapeDtypeStruct((TOP_K, t), jnp.int32),
        jax.ShapeDtypeStruct((N_EXPERTS, V7X_LANES), jnp.int32),
    ]
    out_specs = [
        pl.BlockSpec((ts, d), lambda i, c: (i, 0)),
        pl.BlockSpec((ts, d // 2), lambda i, c: (i, 0)),
        tok_rows(TOP_K), tok_rows(TOP_K), tok_rows(TOP_K),
        const((N_EXPERTS, V7X_LANES)),
    ]
    scratch = [
        pltpu.VMEM((ts + 2 * HALO, d), _BF16),
        pltpu.VMEM((nc, ts, cw), _BF16),
        pltpu.VMEM((nc, ts, cw), _F32),
        pltpu.VMEM((nc, ts, cw), _F32),
        pltpu.VMEM((nc, ts, cw), _F32),
        pltpu.VMEM((ts + 2 * HALO, cw), _F32),
        pltpu.VMEM((ts + 2 * HALO, cw), _F32),
        pltpu.VMEM((N_EXPERTS, V7X_LANES), _F32),
    ]
    kern = functools.partial(_mixer_kernel, ts=ts, cw=cw, nc=nc, tiles_per_seq=seq_len // ts)
    w_parts = [w_in_bf] * N_IN_PARTS
    return pl.pallas_call(
        kern,
        grid=(n_tiles, nc),
        in_specs=in_specs,
        out_specs=out_specs,
        out_shape=out_shape,
        scratch_shapes=scratch,
        compiler_params=pltpu.CompilerParams(
            dimension_semantics=("arbitrary", "arbitrary"),
            vmem_limit_bytes=V7X_VMEM_LIMIT_BYTES),
        name="mixer_router",
    )(x2d, x2d, x2d, *w_parts, conv_a_w, conv_a_b, conv_b_w, conv_b_b, ln_c_g, ln_c_b,
      w_out_a_bf, w_out_b_bf, w_o_bf, ln1_g, ln1_b, wr_hi, wr_lo, rbias, tri)


def _slot_kernel(pstart_ref, eid_ref, rank_ref, slot_ref):
    eid = eid_ref[...]

    def body(e, acc):
        return jnp.where(eid == e, pstart_ref[e], acc)

    slot_ref[...] = lax.fori_loop(0, N_EXPERTS, body, jnp.zeros_like(eid), unroll=8) + rank_ref[...]


def _slot_call(pstart, eid, rank):
    t = eid.shape[1]
    tl = min(SLOT_TILE, t)
    spec = pl.BlockSpec((TOP_K, tl), lambda i, *_: (0, i))
    return pl.pallas_call(
        _slot_kernel,
        grid_spec=pltpu.PrefetchScalarGridSpec(
            num_scalar_prefetch=1, grid=(t // tl,), in_specs=[spec, spec], out_specs=spec),
        out_shape=jax.ShapeDtypeStruct(eid.shape, jnp.int32),
        compiler_params=pltpu.CompilerParams(dimension_semantics=("arbitrary",)),
        name="slot_lookup",
    )(pstart, eid, rank)


def _sc_gather_rows(table, idx):
    n, = idx.shape
    d = table.shape[1]
    n_workers = V7X_SC_CORES * V7X_SC_SUBCORES
    ch = SC_CHUNK_ROWS
    per_worker = n // n_workers
    assert n % (n_workers * ch) == 0
    mesh = plsc.VectorSubcoreMesh(core_axis_name="c", subcore_axis_name="s")

    @functools.partial(
        pl.kernel, mesh=mesh,
        out_type=jax.ShapeDtypeStruct((n, d), table.dtype),
        scratch_types=[pltpu.VMEM((ch,), jnp.int32), pltpu.VMEM((ch, d), table.dtype),
                       pltpu.SemaphoreType.DMA],
        name="sc_gather_rows",
    )
    def gather(table_hbm, idx_hbm, out_hbm, idx_v, rows_v, sem):
        worker = lax.axis_index("s") * V7X_SC_CORES + lax.axis_index("c")
        base = worker * per_worker

        @pl.loop(0, per_worker // ch)
        def _(c):
            off = base + c * ch
            pltpu.sync_copy(idx_hbm.at[pl.ds(off, ch)], idx_v)
            pltpu.async_copy(table_hbm.at[idx_v], rows_v, sem).wait()
            pltpu.sync_copy(rows_v, out_hbm.at[pl.ds(off, ch)])

    return gather(table, idx)


def _sc_dispatch_rows(x, slot_flat, n_rows):
    t, d = x.shape
    n_workers = V7X_SC_CORES * V7X_SC_SUBCORES
    ch = SC_CHUNK_ROWS
    per_worker = t // n_workers
    assert t % (n_workers * ch) == 0 and slot_flat.shape == (TOP_K * t,)
    mesh = plsc.VectorSubcoreMesh(core_axis_name="c", subcore_axis_name="s")

    @functools.partial(
        pl.kernel, mesh=mesh,
        out_type=jax.ShapeDtypeStruct((n_rows, d), x.dtype),
        scratch_types=[pltpu.VMEM((ch,), jnp.int32)] * TOP_K + [
            pltpu.VMEM((ch, d), x.dtype), pltpu.SemaphoreType.DMA, pltpu.SemaphoreType.DMA],
        name="sc_dispatch_rows",
    )
    def dispatch(x_hbm, slot_hbm, out_hbm, *scratch):
        idx_vs, rows_v, idx_sem, row_sem = scratch[:TOP_K], scratch[TOP_K], scratch[TOP_K + 1], scratch[TOP_K + 2]
        worker = lax.axis_index("s") * V7X_SC_CORES + lax.axis_index("c")
        base = worker * per_worker

        @pl.loop(0, per_worker // ch)
        def _(c):
            off = base + c * ch
            idx_copies = [pltpu.async_copy(slot_hbm.at[pl.ds(k * t + off, ch)], idx_vs[k], idx_sem)
                          for k in range(TOP_K)]
            pltpu.sync_copy(x_hbm.at[pl.ds(off, ch)], rows_v)
            for cp in idx_copies:
                cp.wait()
            row_copies = [pltpu.async_copy(rows_v, out_hbm.at[idx_vs[k]], row_sem) for k in range(TOP_K)]
            for cp in row_copies:
                cp.wait()

    return dispatch(x, slot_flat)


def _experts_kernel(blk_e_ref, blk_valid_ref, n_used_ref, xs_ref, wg_ref, wu_ref, wd_ref, ys_ref):
    del blk_e_ref
    b = pl.program_id(0)

    @pl.when(b < n_used_ref[0])
    def _():
        row = lax.broadcasted_iota(jnp.int32, xs_ref.shape, 0)
        x_hi, x_lo = _unpack_bf16_pairs(jnp.where(row < blk_valid_ref[b], xs_ref[...], 0))
        x_hi, x_lo = x_hi.astype(_BF16), x_lo.astype(_BF16)
        half = xs_ref.shape[1]
        wg = wg_ref[0].astype(_BF16)
        wu = wu_ref[0].astype(_BF16)
        g = _dot(x_hi, wg[:half]) + _dot(x_lo, wg[half:])
        u = _dot(x_hi, wu[:half]) + _dot(x_lo, wu[half:])
        h = (g * _sigmoid(g) * u).astype(_BF16)
        ys_ref[...] = _pack_bf16_pairs(_dot(h, wd_ref[0].astype(_BF16)))


def _experts_call(blk_e, blk_valid, n_used, xs, w_gate, w_up, w_down):
    n_rows, dp = xs.shape
    bm = EXPERT_ROWS
    d, f = w_gate.shape[-2:]
    assert d == 2 * dp

    def row_block(b, blk_e_ref, blk_valid_ref, n_used_ref):
        return (jnp.minimum(b, n_used_ref[0] - 1), 0)

    def expert_block(b, blk_e_ref, blk_valid_ref, n_used_ref):
        return (blk_e_ref[b], 0, 0)

    grid_spec = pltpu.PrefetchScalarGridSpec(
        num_scalar_prefetch=3,
        grid=(n_rows // bm,),
        in_specs=[
            pl.BlockSpec((bm, dp), row_block),
            pl.BlockSpec((1, d, f), expert_block),
            pl.BlockSpec((1, d, f), expert_block),
            pl.BlockSpec((1, f, d), expert_block),
        ],
        out_specs=pl.BlockSpec((bm, dp), row_block),
    )
    return pl.pallas_call(
        _experts_kernel,
        grid_spec=grid_spec,
        out_shape=jax.ShapeDtypeStruct((n_rows, dp), jnp.int32),
        compiler_params=pltpu.CompilerParams(
            dimension_semantics=("arbitrary",),
            vmem_limit_bytes=V7X_VMEM_LIMIT_BYTES),
        name="expert_ffn",
    )(blk_e, blk_valid, n_used, xs, w_gate, w_up, w_down)


def _combine_kernel(wts_ref, x1_ref, rows_ref, wsg_ref, wsu_ref, wsd_ref, g_ref, b_ref, out_ref):
    x1 = x1_ref[...]
    xb = x1.astype(_BF16)
    g = _dot(xb, wsg_ref[...])
    u = _dot(xb, wsu_ref[...])
    acc = ALPHA * x1 + _dot((g * _sigmoid(g) * u).astype(_BF16), wsd_ref[...])
    half = rows_ref.shape[2]
    acc_hi, acc_lo = acc[:, :half], acc[:, half:]
    w = wts_ref[...]
    for k in range(TOP_K):
        r_hi, r_lo = _unpack_bf16_pairs(rows_ref[k])
        acc_hi = acc_hi + r_hi * w[:, k:k + 1]
        acc_lo = acc_lo + r_lo * w[:, k:k + 1]
    acc = jnp.concatenate([acc_hi, acc_lo], axis=1)
    out_ref[...] = _layer_norm_rows(acc, g_ref[...], b_ref[...])


def _combine_call(wts_tk, x1, rows, ws_gate_bf, ws_up_bf, ws_down_bf, ln2_g, ln2_b):
    t, d = x1.shape
    tt = TOKEN_TILE
    fs = ws_gate_bf.shape[-1]

    def const(shape):
        return pl.BlockSpec(shape, lambda i: (0,) * len(shape))

    return pl.pallas_call(
        _combine_kernel,
        grid=(t // tt,),
        in_specs=[
            pl.BlockSpec((tt, TOP_K), lambda i: (i, 0)),
            pl.BlockSpec((tt, d), lambda i: (i, 0)),
            pl.BlockSpec((TOP_K, tt, d // 2), lambda i: (0, i, 0)),
            const((d, fs)), const((d, fs)), const((fs, d)),
            const((1, d)), const((1, d)),
        ],
        out_specs=pl.BlockSpec((tt, d), lambda i: (i, 0)),
        out_shape=jax.ShapeDtypeStruct((t, d), _F32),
        compiler_params=pltpu.CompilerParams(
            dimension_semantics=("arbitrary",),
            vmem_limit_bytes=V7X_VMEM_LIMIT_BYTES),
        name="combine_shared_norm",
    )(wts_tk, x1, rows, ws_gate_bf, ws_up_bf, ws_down_bf, ln2_g, ln2_b)


def _trunk(x, p):
    bsz, seq, d = x.shape
    t = bsz * seq
    x1, x1p, eid, wts, rank, cnt = _mixer_call(
        x.reshape(t, d), seq, p["w_in"], p["conv_a_w"], p["conv_a_b"], p["conv_b_w"], p["conv_b_b"],
        p["ln_c_g"], p["ln_c_b"], p["w_out_a"], p["w_out_b"], p["w_o"], p["ln1_g"], p["ln1_b"],
        p["wr_hi"], p["wr_lo"], p["rbias"], p["tri"])

    bm = EXPERT_ROWS
    counts = cnt[:, 0]
    pcounts = (counts + bm - 1) // bm * bm
    pend = jnp.cumsum(pcounts)
    pstart = pend - pcounts
    n_blocks = (t * TOP_K + N_EXPERTS * (bm - 1)) // bm
    n_used = jnp.maximum(pend[-1] // bm, 1).astype(jnp.int32)
    blk_start = jnp.minimum(jnp.arange(n_blocks, dtype=jnp.int32), n_used - 1) * bm
    blk_e = jnp.minimum(jnp.searchsorted(pend, blk_start, side="right"), N_EXPERTS - 1).astype(jnp.int32)
    blk_valid = jnp.clip(counts[blk_e] - (blk_start - pstart[blk_e]), 0, bm).astype(jnp.int32)
    pstart = pstart.astype(jnp.int32)

    slot = _slot_call(pstart, eid, rank).reshape(TOP_K * t)
    xs = _sc_dispatch_rows(x1p, slot, n_blocks * bm)
    ys = _experts_call(blk_e, blk_valid, n_used.reshape(1), xs, p["w_gate"], p["w_up"], p["w_down"])
    rows = _sc_gather_rows(ys, slot).reshape(TOP_K, t, d // 2)
    out = _combine_call(wts.T, x1, rows, p["ws_gate"], p["ws_up"], p["ws_down"], p["ln2_g"], p["ln2_b"])
    return out.reshape(bsz, seq, d)


def _prepare_params(w_in, conv_a_w, conv_a_b, w_out_a, conv_b_w, conv_b_b, ln_c_g, ln_c_b,
                    w_out_b, w_o, ln1_g, ln1_b, w_router, router_bias, w_gate, w_up, w_down,
                    ws_gate, ws_up, ws_down, ln2_g, ln2_b):
    assert w_in.shape[0] == DEPTH == 1
    d = w_in.shape[1]
    wr = w_router[0].T
    wr_hi = wr.astype(_BF16)
    ts = SEQ_TILE
    return dict(
        w_in=w_in[0].astype(_BF16),
        conv_a_w=conv_a_w[0], conv_a_b=conv_a_b[0].reshape(1, d),
        conv_b_w=conv_b_w[0], conv_b_b=conv_b_b[0].reshape(1, d),
        ln_c_g=ln_c_g[0].reshape(1, d), ln_c_b=ln_c_b[0].reshape(1, d),
        w_out_a=w_out_a[0].astype(_BF16), w_out_b=w_out_b[0].astype(_BF16), w_o=w_o[0].astype(_BF16),
        ln1_g=ln1_g[0].reshape(1, d), ln1_b=ln1_b[0].reshape(1, d),
        wr_hi=wr_hi, wr_lo=(wr - wr_hi.astype(_F32)).astype(_BF16),
        rbias=router_bias[0].reshape(N_EXPERTS, 1),
        tri=(jnp.arange(ts)[:, None] < jnp.arange(ts)[None, :]).astype(_BF16),
        w_gate=w_gate[0], w_up=w_up[0], w_down=w_down[0],
        ws_gate=ws_gate[0].astype(_BF16), ws_up=ws_up[0].astype(_BF16), ws_down=ws_down[0].astype(_BF16),
        ln2_g=ln2_g[0].reshape(1, d), ln2_b=ln2_b[0].reshape(1, d),
    )


def kernel(x_prompt, x_sample, w_in, conv_a_w, conv_a_b, w_out_a, conv_b_w, conv_b_b, ln_c_g, ln_c_b,
           w_out_b, w_o, ln1_g, ln1_b, w_router, router_bias, w_gate, w_up, w_down, ws_gate, ws_up,
           ws_down, ln2_g, ln2_b):
    p = _prepare_params(w_in, conv_a_w, conv_a_b, w_out_a, conv_b_w, conv_b_b, ln_c_g, ln_c_b,
                        w_out_b, w_o, ln1_g, ln1_b, w_router, router_bias, w_gate, w_up, w_down,
                        ws_gate, ws_up, ws_down, ln2_g, ln2_b)
    return (_trunk(x_prompt, p), _trunk(x_sample, p))
```

```python
import functools

import jax
import jax.numpy as jnp
from jax import lax
from jax.experimental import pallas as pl
from jax.experimental.pallas import tpu as pltpu
from jax.experimental.pallas import tpu_sc as plsc

K_SHORT = 3
K_CONF = 31
N_EXPERTS = 256
TOP_K = 8
N_GROUPS = 8
TOPK_GROUPS = 4
GROUP_SIZE = N_EXPERTS // N_GROUPS
ROUTE_SCALE = 2.5
LN_EPS = 1e-5
DEPTH = 1
ALPHA = (2.0 * DEPTH) ** 0.25
N_IN_PARTS = 7

V7X_LANES = 128
V7X_VMEM_LIMIT_BYTES = 56 * 1024 * 1024

HALO = 16
SEQ_TILE = 512
COL_CHUNK = 256
EXPERT_ROWS = 256
TOKEN_TILE = 256
SLOT_TILE = 2048

V7X_SC_CORES = 2
V7X_SC_SUBCORES = 16
SC_CHUNK_ROWS = 64

_F32 = jnp.float32
_BF16 = jnp.bfloat16
_HIGH_HALF = -65536


def _dot(a, b):
    return jnp.dot(a, b, preferred_element_type=_F32)


def _dot_nt(a, b):
    return lax.dot_general(a, b, (((1,), (1,)), ((), ())), preferred_element_type=_F32)


def _sigmoid(v):
    return 1.0 / (1.0 + jnp.exp(-v))


def _pack_bf16_pairs(v):
    h = v.shape[1] // 2
    hi = lax.bitcast_convert_type(v[:, :h].astype(_BF16).astype(_F32), jnp.int32)
    lo = lax.bitcast_convert_type(v[:, h:].astype(_BF16).astype(_F32), jnp.int32)
    return (hi & _HIGH_HALF) | lax.shift_right_logical(lo, 16)


def _unpack_bf16_pairs(p):
    hi = lax.bitcast_convert_type(p & _HIGH_HALF, _F32)
    lo = lax.bitcast_convert_type(lax.shift_left(p, 16), _F32)
    return hi, lo


def _layer_norm_rows(v, g, b):
    mu = jnp.mean(v, axis=-1, keepdims=True)
    d = v - mu
    var = jnp.mean(d * d, axis=-1, keepdims=True)
    return d * lax.rsqrt(var + LN_EPS) * g + b


def _route(logits_t, rbias, ts):
    scores = _sigmoid(logits_t)
    biased = scores + rbias
    neg = -jnp.inf
    iota_g = lax.broadcasted_iota(jnp.int32, (GROUP_SIZE, ts), 0)
    gscore = []
    for g in range(N_GROUPS):
        v = biased[g * GROUP_SIZE:(g + 1) * GROUP_SIZE]
        m1 = jnp.max(v, axis=0, keepdims=True)
        i1 = jnp.min(jnp.where(v == m1, iota_g, GROUP_SIZE), axis=0, keepdims=True)
        m2 = jnp.max(jnp.where(iota_g == i1, neg, v), axis=0, keepdims=True)
        gscore.append(m1 + m2)
    masked = []
    for g in range(N_GROUPS):
        ahead = jnp.zeros((1, ts), jnp.int32)
        for h in range(N_GROUPS):
            if h == g:
                continue
            before = gscore[h] > gscore[g]
            if h < g:
                before = before | (gscore[h] == gscore[g])
            ahead = ahead + before.astype(jnp.int32)
        keep = ahead < TOPK_GROUPS
        masked.append(jnp.where(keep, biased[g * GROUP_SIZE:(g + 1) * GROUP_SIZE], neg))
    v = jnp.concatenate(masked, axis=0)

    iota_e = lax.broadcasted_iota(jnp.int32, (N_EXPERTS, ts), 0)
    ids, wts = [], []
    onehot = jnp.zeros((N_EXPERTS, ts), _F32)
    for _ in range(TOP_K):
        m = jnp.max(v, axis=0, keepdims=True)
        idx = jnp.min(jnp.where(v == m, iota_e, N_EXPERTS), axis=0, keepdims=True)
        hit = iota_e == idx
        wts.append(jnp.sum(jnp.where(hit, scores, 0.0), axis=0, keepdims=True))
        ids.append(idx)
        v = jnp.where(hit, neg, v)
        onehot = jnp.where(hit, 1.0, onehot)
    total = wts[0]
    for w in wts[1:]:
        total = total + w
    wts = [w / total * ROUTE_SCALE for w in wts]
    return ids, wts, onehot


def _mixer_kernel(xm_ref, xp_ref, xn_ref,
                  wh_ref, wb_ref, wc_ref, wv_ref, wg_ref, wga_ref, wgb_ref,
                  caw_ref, cab_ref, cbw_ref, cbb_ref, lncg_ref, lncb_ref,
                  woa_ref, wob_ref, wo_ref, ln1g_ref, ln1b_ref,
                  wrh_ref, wrl_ref, rbias_ref, tri_ref,
                  x1_ref, x1p_ref, eid_ref, wts_ref, rank_ref, cnt_ref,
                  xe_s, pa_s, z_s, ga_s, gb_s, ch_s, zin_s, base_s,
                  *, ts, cw, nc, tiles_per_seq):
    i = pl.program_id(0)
    c = pl.program_id(1)
    pos = lax.rem(i, tiles_per_seq)

    @pl.when((i == 0) & (c == 0))
    def _():
        base_s[...] = jnp.zeros_like(base_s)

    @pl.when(c == 0)
    def _():
        keep_prev = (pos > 0).astype(_F32)
        keep_next = (pos < tiles_per_seq - 1).astype(_F32)
        xe_s[0:HALO, :] = (xp_ref[...] * keep_prev).astype(_BF16)
        xe_s[HALO:HALO + ts, :] = xm_ref[...].astype(_BF16)
        xe_s[HALO + ts:, :] = (xn_ref[...] * keep_next).astype(_BF16)

    xe = xe_s[...]
    xc = xe_s[HALO:HALO + ts, :]

    ch_s[...] = _dot(xe, wh_ref[...]) * _dot(xe, wc_ref[...])
    conv_a = cab_ref[...]
    for j in range(K_SHORT):
        conv_a = conv_a + caw_ref[j:j + 1, :] * ch_s[pl.ds(HALO - K_SHORT // 2 + j, ts), :]
    pa_s[c] = (_dot(xc, wb_ref[...]) * conv_a).astype(_BF16)

    zin_s[...] = _dot(xe, wv_ref[...]) * _sigmoid(_dot(xe, wg_ref[...]))
    conv_b = cbb_ref[...]
    for j in range(K_CONF):
        conv_b = conv_b + cbw_ref[j:j + 1, :] * zin_s[pl.ds(HALO - K_CONF // 2 + j, ts), :]
    z_s[c] = conv_b

    ga_s[c] = _sigmoid(_dot(xc, wga_ref[...]))
    gb_s[c] = _sigmoid(_dot(xc, wgb_ref[...]))

    @pl.when(c == nc - 1)
    def _():
        d_model = nc * cw
        mu = sum(jnp.sum(z_s[k], axis=-1, keepdims=True) for k in range(nc)) / d_model
        var = sum(jnp.sum((z_s[k] - mu) ** 2, axis=-1, keepdims=True) for k in range(nc)) / d_model
        inv = lax.rsqrt(var + LN_EPS)
        y_a = jnp.zeros((ts, d_model), _F32)
        y_b = jnp.zeros((ts, d_model), _F32)
        for k in range(nc):
            cols = slice(k * cw, (k + 1) * cw)
            zn = (z_s[k] - mu) * inv * lncg_ref[:, cols] + lncb_ref[:, cols]
            sw = zn * _sigmoid(zn)
            y_b = y_b + _dot(sw.astype(_BF16), wob_ref[cols, :])
            y_a = y_a + _dot(pa_s[k], woa_ref[cols, :])
        mix = jnp.zeros((ts, d_model), _F32)
        for k in range(nc):
            cols = slice(k * cw, (k + 1) * cw)
            merged = ga_s[k] * y_a[:, cols] + gb_s[k] * y_b[:, cols]
            mix = mix + _dot(merged.astype(_BF16), wo_ref[cols, :])
        x1 = _layer_norm_rows(ALPHA * xm_ref[...] + mix, ln1g_ref[...], ln1b_ref[...])
        x1_ref[...] = x1
        x1p_ref[...] = _pack_bf16_pairs(x1)

        xh = x1.astype(_BF16)
        xl = (x1 - xh.astype(_F32)).astype(_BF16)
        logits_t = _dot_nt(wrh_ref[...], xh) + (_dot_nt(wrl_ref[...], xh) + _dot_nt(wrh_ref[...], xl))
        ids, wts, onehot = _route(logits_t, rbias_ref[...], ts)

        before = _dot(onehot.astype(_BF16), tri_ref[...]) + base_s[:, 0:1]
        iota_e = lax.broadcasted_iota(jnp.int32, (N_EXPERTS, ts), 0)
        for k in range(TOP_K):
            eid_ref[k:k + 1, :] = ids[k]
            wts_ref[k:k + 1, :] = wts[k]
            r = jnp.sum(jnp.where(iota_e == ids[k], before, 0.0), axis=0, keepdims=True)
            rank_ref[k:k + 1, :] = r.astype(jnp.int32)
        base_s[...] = base_s[...] + jnp.sum(onehot, axis=1, keepdims=True)
        cnt_ref[...] = base_s[...].astype(jnp.int32)


def _mixer_call(x2d, seq_len, w_in_bf, conv_a_w, conv_a_b, conv_b_w, conv_b_b, ln_c_g, ln_c_b,
                w_out_a_bf, w_out_b_bf, w_o_bf, ln1_g, ln1_b, wr_hi, wr_lo, rbias, tri):
    t, d = x2d.shape
    ts, cw = SEQ_TILE, COL_CHUNK
    nc = d // cw
    assert seq_len % ts == 0 and d % cw == 0 and ts % HALO == 0
    n_tiles = t // ts
    halo_blocks_per_tile = ts // HALO
    n_halo_blocks = t // HALO

    def part_spec(p):
        return pl.BlockSpec((d, cw), lambda i, c, p=p: (0, p * nc + c))

    def chunk_rows(rows):
        return pl.BlockSpec((rows, cw), lambda i, c: (0, c))

    def const(shape):
        return pl.BlockSpec(shape, lambda i, c: (0,) * len(shape))

    tok_rows = lambda rows: pl.BlockSpec((rows, ts), lambda i, c: (0, i))

    in_specs = [
        pl.BlockSpec((ts, d), lambda i, c: (i, 0)),
        pl.BlockSpec((HALO, d), lambda i, c: (jnp.maximum(i * halo_blocks_per_tile - 1, 0), 0)),
        pl.BlockSpec((HALO, d),
                     lambda i, c: (jnp.minimum((i + 1) * halo_blocks_per_tile, n_halo_blocks - 1), 0)),
    ] + [part_spec(p) for p in range(N_IN_PARTS)] + [
        chunk_rows(K_SHORT), chunk_rows(1), chunk_rows(K_CONF), chunk_rows(1),
        const((1, d)), const((1, d)),
        const((d, d)), const((d, d)), const((d, d)),
        const((1, d)), const((1, d)),
        const((N_EXPERTS, d)), const((N_EXPERTS, d)), const((N_EXPERTS, 1)),
        const((ts, ts)),
    ]
    out_shape = [
        jax.ShapeDtypeStruct((t, d), _F32),
        jax.ShapeDtypeStruct((t, d // 2), jnp.int32),
        jax.ShapeDtypeStruct((TOP_K, t), jnp.int32),
        jax.ShapeDtypeStruct((TOP_K, t), _F32),
        jax.ShapeDtypeStruct((TOP_K, t), jnp.int32),
        jax.ShapeDtypeStruct((N_EXPERTS, V7X_LANES), jnp.int32),
    ]
    out_specs = [
        pl.BlockSpec((ts, d), lambda i, c: (i, 0)),
        pl.BlockSpec((ts, d // 2), lambda i, c: (i, 0)),
        tok_rows(TOP_K), tok_rows(TOP_K), tok_rows(TOP_K),
        const((N_EXPERTS, V7X_LANES)),
    ]
    scratch = [
        pltpu.VMEM((ts + 2 * HALO, d), _BF16),
        pltpu.VMEM((nc, ts, cw), _BF16),
        pltpu.VMEM((nc, ts, cw), _F32),
        pltpu.VMEM((nc, ts, cw), _F32),
        pltpu.VMEM((nc, ts, cw), _F32),
        pltpu.VMEM((ts + 2 * HALO, cw), _F32),
        pltpu.VMEM((ts + 2 * HALO, cw), _F32),
        pltpu.VMEM((N_EXPERTS, V7X_LANES), _F32),
    ]
    kern = functools.partial(_mixer_kernel, ts=ts, cw=cw, nc=nc, tiles_per_seq=seq_len // ts)
    w_parts = [w_in_bf] * N_IN_PARTS
    return pl.pallas_call(
        kern,
        grid=(n_tiles, nc),
        in_specs=in_specs,
        out_specs=out_specs,
        out_shape=out_shape,
        scratch_shapes=scratch,
        compiler_params=pltpu.CompilerParams(
            dimension_semantics=("arbitrary", "arbitrary"),
            vmem_limit_bytes=V7X_VMEM_LIMIT_BYTES),
        name="mixer_router",
    )(x2d, x2d, x2d, *w_parts, conv_a_w, conv_a_b, conv_b_w, conv_b_b, ln_c_g, ln_c_b,
      w_out_a_bf, w_out_b_bf, w_o_bf, ln1_g, ln1_b, wr_hi, wr_lo, rbias, tri)


def _slot_kernel(pstart_ref, eid_ref, rank_ref, slot_ref):
    eid = eid_ref[...]

    def body(e, acc):
        return jnp.where(eid == e, pstart_ref[e], acc)

    slot_ref[...] = lax.fori_loop(0, N_EXPERTS, body, jnp.zeros_like(eid), unroll=8) + rank_ref[...]


def _slot_call(pstart, eid, rank):
    t = eid.shape[1]
    tl = min(SLOT_TILE, t)
    spec = pl.BlockSpec((TOP_K, tl), lambda i, *_: (0, i))
    return pl.pallas_call(
        _slot_kernel,
        grid_spec=pltpu.PrefetchScalarGridSpec(
            num_scalar_prefetch=1, grid=(t // tl,), in_specs=[spec, spec], out_specs=spec),
        out_shape=jax.ShapeDtypeStruct(eid.shape, jnp.int32),
        compiler_params=pltpu.CompilerParams(dimension_semantics=("arbitrary",)),
        name="slot_lookup",
    )(pstart, eid, rank)


def _sc_gather_rows(table, idx):
    n, = idx.shape
    d = table.shape[1]
    n_workers = V7X_SC_CORES * V7X_SC_SUBCORES
    ch = SC_CHUNK_ROWS
    per_worker = n // n_workers
    assert n % (n_workers * ch) == 0
    mesh = plsc.VectorSubcoreMesh(core_axis_name="c", subcore_axis_name="s")

    @functools.partial(
        pl.kernel, mesh=mesh,
        out_type=jax.ShapeDtypeStruct((n, d), table.dtype),
        scratch_types=[pltpu.VMEM((ch,), jnp.int32), pltpu.VMEM((ch, d), table.dtype),
                       pltpu.SemaphoreType.DMA],
        name="sc_gather_rows",
    )
    def gather(table_hbm, idx_hbm, out_hbm, idx_v, rows_v, sem):
        worker = lax.axis_index("s") * V7X_SC_CORES + lax.axis_index("c")
        base = worker * per_worker

        @pl.loop(0, per_worker // ch)
        def _(c):
            off = base + c * ch
            pltpu.sync_copy(idx_hbm.at[pl.ds(off, ch)], idx_v)
            pltpu.async_copy(table_hbm.at[idx_v], rows_v, sem).wait()
            pltpu.sync_copy(rows_v, out_hbm.at[pl.ds(off, ch)])

    return gather(table, idx)


def _sc_dispatch_rows(x, slot_flat, n_rows):
    t, d = x.shape
    n_workers = V7X_SC_CORES * V7X_SC_SUBCORES
    ch = SC_CHUNK_ROWS
    per_worker = t // n_workers
    assert t % (n_workers * ch) == 0 and slot_flat.shape == (TOP_K * t,)
    mesh = plsc.VectorSubcoreMesh(core_axis_name="c", subcore_axis_name="s")

    @functools.partial(
        pl.kernel, mesh=mesh,
        out_type=jax.ShapeDtypeStruct((n_rows, d), x.dtype),
        scratch_types=[pltpu.VMEM((ch,), jnp.int32)] * TOP_K + [
            pltpu.VMEM((ch, d), x.dtype), pltpu.SemaphoreType.DMA, pltpu.SemaphoreType.DMA],
        name="sc_dispatch_rows",
    )
    def dispatch(x_hbm, slot_hbm, out_hbm, *scratch):
        idx_vs, rows_v, idx_sem, row_sem = scratch[:TOP_K], scratch[TOP_K], scratch[TOP_K + 1], scratch[TOP_K + 2]
        worker = lax.axis_index("s") * V7X_SC_CORES + lax.axis_index("c")
        base = worker * per_worker

        @pl.loop(0, per_worker // ch)
        def _(c):
            off = base + c * ch
            idx_copies = [pltpu.async_copy(slot_hbm.at[pl.ds(k * t + off, ch)], idx_vs[k], idx_sem)
                          for k in range(TOP_K)]
            pltpu.sync_copy(x_hbm.at[pl.ds(off, ch)], rows_v)
            for cp in idx_copies:
                cp.wait()
            row_copies = [pltpu.async_copy(rows_v, out_hbm.at[idx_vs[k]], row_sem) for k in range(TOP_K)]
            for cp in row_copies:
                cp.wait()

    return dispatch(x, slot_flat)


def _experts_kernel(blk_e_ref, blk_valid_ref, first_ref, next_e_ref, n_used_ref,
                    xs_hbm, wg_hbm, wu_hbm, wd_hbm, ys_hbm,
                    xbuf, ybuf, wg_f, wu_f, wd_f, wg_b, wu_b, wd_b, in_sem, out_sem, w_sem, *, bm):
    n_used = n_used_ref[0]
    half = xbuf.shape[2]

    def x_copy(b, slot):
        return pltpu.make_async_copy(xs_hbm.at[pl.ds(b * bm, bm)], xbuf.at[slot], in_sem.at[slot])

    def y_copy(b, slot):
        return pltpu.make_async_copy(ybuf.at[slot], ys_hbm.at[pl.ds(b * bm, bm)], out_sem.at[slot])

    def w_copies(e, wslot):
        return [pltpu.make_async_copy(src.at[e], dst.at[wslot], w_sem.at[wslot])
                for src, dst in ((wg_hbm, wg_f), (wu_hbm, wu_f), (wd_hbm, wd_f))]

    x_copy(0, 0).start()
    for cp in w_copies(blk_e_ref[0], 0):
        cp.start()

    def block(b, w_next):
        slot = lax.rem(b, 2)

        @pl.when(b + 1 < n_used)
        def _():
            x_copy(b + 1, 1 - slot).start()

        is_first = first_ref[b] == 1

        @pl.when(is_first)
        def _():
            for cp in w_copies(blk_e_ref[b], w_next):
                cp.wait()
            wg_b[...] = wg_f[w_next].astype(_BF16)
            wu_b[...] = wu_f[w_next].astype(_BF16)
            wd_b[...] = wd_f[w_next].astype(_BF16)

            @pl.when(next_e_ref[b] >= 0)
            def _():
                for cp in w_copies(next_e_ref[b], 1 - w_next):
                    cp.start()

        x_copy(b, slot).wait()

        @pl.when(b >= 2)
        def _():
            y_copy(b - 2, slot).wait()

        row = lax.broadcasted_iota(jnp.int32, (bm, half), 0)
        x_hi, x_lo = _unpack_bf16_pairs(jnp.where(row < blk_valid_ref[b], xbuf[slot], 0))
        x_hi, x_lo = x_hi.astype(_BF16), x_lo.astype(_BF16)
        g = _dot(x_hi, wg_b[:half, :]) + _dot(x_lo, wg_b[half:, :])
        u = _dot(x_hi, wu_b[:half, :]) + _dot(x_lo, wu_b[half:, :])
        h = (g * _sigmoid(g) * u).astype(_BF16)
        ybuf[slot] = _pack_bf16_pairs(_dot(h, wd_b[...]))
        y_copy(b, slot).start()
        return jnp.where(is_first, 1 - w_next, w_next)

    lax.fori_loop(0, n_used, block, jnp.int32(0))

    @pl.when(n_used >= 2)
    def _():
        y_copy(n_used - 2, lax.rem(n_used, 2)).wait()

    y_copy(n_used - 1, lax.rem(n_used - 1, 2)).wait()


def _experts_call(blk_e, blk_valid, blk_first, blk_next_e, n_used, xs, w_gate, w_up, w_down):
    n_rows, dp = xs.shape
    bm = EXPERT_ROWS
    d, f = w_gate.shape[-2:]
    assert d == 2 * dp
    any_spec = pl.BlockSpec(memory_space=pl.ANY)
    grid_spec = pltpu.PrefetchScalarGridSpec(
        num_scalar_prefetch=5,
        grid=(1,),
        in_specs=[any_spec] * 4,
        out_specs=any_spec,
        scratch_shapes=[
            pltpu.VMEM((2, bm, dp), jnp.int32), pltpu.VMEM((2, bm, dp), jnp.int32),
            pltpu.VMEM((2, d, f), _F32), pltpu.VMEM((2, d, f), _F32), pltpu.VMEM((2, f, d), _F32),
            pltpu.VMEM((d, f), _BF16), pltpu.VMEM((d, f), _BF16), pltpu.VMEM((f, d), _BF16),
            pltpu.SemaphoreType.DMA((2,)), pltpu.SemaphoreType.DMA((2,)), pltpu.SemaphoreType.DMA((2,)),
        ],
    )
    return pl.pallas_call(
        functools.partial(_experts_kernel, bm=bm),
        grid_spec=grid_spec,
        out_shape=jax.ShapeDtypeStruct((n_rows, dp), jnp.int32),
        compiler_params=pltpu.CompilerParams(
            dimension_semantics=("arbitrary",),
            vmem_limit_bytes=V7X_VMEM_LIMIT_BYTES),
        name="expert_ffn",
    )(blk_e, blk_valid, blk_first, blk_next_e, n_used, xs, w_gate, w_up, w_down)


def _combine_kernel(wts_ref, x1_ref, rows_ref, wsg_ref, wsu_ref, wsd_ref, g_ref, b_ref, out_ref):
    x1 = x1_ref[...]
    xb = x1.astype(_BF16)
    g = _dot(xb, wsg_ref[...])
    u = _dot(xb, wsu_ref[...])
    acc = ALPHA * x1 + _dot((g * _sigmoid(g) * u).astype(_BF16), wsd_ref[...])
    half = rows_ref.shape[2]
    acc_hi, acc_lo = acc[:, :half], acc[:, half:]
    w = wts_ref[...]
    for k in range(TOP_K):
        r_hi, r_lo = _unpack_bf16_pairs(rows_ref[k])
        acc_hi = acc_hi + r_hi * w[:, k:k + 1]
        acc_lo = acc_lo + r_lo * w[:, k:k + 1]
    acc = jnp.concatenate([acc_hi, acc_lo], axis=1)
    out_ref[...] = _layer_norm_rows(acc, g_ref[...], b_ref[...])


def _combine_call(wts_tk, x1, rows, ws_gate_bf, ws_up_bf, ws_down_bf, ln2_g, ln2_b):
    t, d = x1.shape
    tt = TOKEN_TILE
    fs = ws_gate_bf.shape[-1]

    def const(shape):
        return pl.BlockSpec(shape, lambda i: (0,) * len(shape))

    return pl.pallas_call(
        _combine_kernel,
        grid=(t // tt,),
        in_specs=[
            pl.BlockSpec((tt, TOP_K), lambda i: (i, 0)),
            pl.BlockSpec((tt, d), lambda i: (i, 0)),
            pl.BlockSpec((TOP_K, tt, d // 2), lambda i: (0, i, 0)),
            const((d, fs)), const((d, fs)), const((fs, d)),
            const((1, d)), const((1, d)),
        ],
        out_specs=pl.BlockSpec((tt, d), lambda i: (i, 0)),
        out_shape=jax.ShapeDtypeStruct((t, d), _F32),
        compiler_params=pltpu.CompilerParams(
            dimension_semantics=("arbitrary",),
            vmem_limit_bytes=V7X_VMEM_LIMIT_BYTES),
        name="combine_shared_norm",
    )(wts_tk, x1, rows, ws_gate_bf, ws_up_bf, ws_down_bf, ln2_g, ln2_b)


def _trunk(x, p):
    bsz, seq, d = x.shape
    t = bsz * seq
    x1, x1p, eid, wts, rank, cnt = _mixer_call(
        x.reshape(t, d), seq, p["w_in"], p["conv_a_w"], p["conv_a_b"], p["conv_b_w"], p["conv_b_b"],
        p["ln_c_g"], p["ln_c_b"], p["w_out_a"], p["w_out_b"], p["w_o"], p["ln1_g"], p["ln1_b"],
        p["wr_hi"], p["wr_lo"], p["rbias"], p["tri"])

    bm = EXPERT_ROWS
    counts = cnt[:, 0]
    pcounts = (counts + bm - 1) // bm * bm
    pend = jnp.cumsum(pcounts)
    pstart = pend - pcounts
    n_blocks = (t * TOP_K + N_EXPERTS * (bm - 1)) // bm
    n_used = jnp.maximum(pend[-1] // bm, 1).astype(jnp.int32)
    blk_start = jnp.minimum(jnp.arange(n_blocks, dtype=jnp.int32), n_used - 1) * bm
    blk_e = jnp.minimum(jnp.sum(pend[None, :] <= blk_start[:, None], axis=1), N_EXPERTS - 1).astype(jnp.int32)
    blk_valid = jnp.clip(counts[blk_e] - (blk_start - pstart[blk_e]), 0, bm).astype(jnp.int32)
    blk_first = (blk_start == pstart[blk_e]).astype(jnp.int32)
    owner = jnp.where(counts > 0, jnp.arange(N_EXPERTS, dtype=jnp.int32), N_EXPERTS)
    next_owner = lax.cummin(jnp.concatenate([owner[1:], jnp.full((1,), N_EXPERTS, jnp.int32)]), reverse=True)
    blk_next_e = jnp.where(next_owner < N_EXPERTS, next_owner, -1)[blk_e].astype(jnp.int32)
    pstart = pstart.astype(jnp.int32)

    slot = _slot_call(pstart, eid, rank).reshape(TOP_K * t)
    xs = _sc_dispatch_rows(x1p, slot, n_blocks * bm)
    ys = _experts_call(blk_e, blk_valid, blk_first, blk_next_e, n_used.reshape(1), xs,
                       p["w_gate"], p["w_up"], p["w_down"])
    rows = _sc_gather_rows(ys, slot).reshape(TOP_K, t, d // 2)
    out = _combine_call(wts.T, x1, rows, p["ws_gate"], p["ws_up"], p["ws_down"], p["ln2_g"], p["ln2_b"])
    return out.reshape(bsz, seq, d)


def _prepare_params(w_in, conv_a_w, conv_a_b, w_out_a, conv_b_w, conv_b_b, ln_c_g, ln_c_b,
                    w_out_b, w_o, ln1_g, ln1_b, w_router, router_bias, w_gate, w_up, w_down,
                    ws_gate, ws_up, ws_down, ln2_g, ln2_b):
    assert w_in.shape[0] == DEPTH == 1
    d = w_in.shape[1]
    wr = w_router[0].T
    wr_hi = wr.astype(_BF16)
    ts = SEQ_TILE
    return dict(
        w_in=w_in[0].astype(_BF16),
        conv_a_w=conv_a_w[0], conv_a_b=conv_a_b[0].reshape(1, d),
        conv_b_w=conv_b_w[0], conv_b_b=conv_b_b[0].reshape(1, d),
        ln_c_g=ln_c_g[0].reshape(1, d), ln_c_b=ln_c_b[0].reshape(1, d),
        w_out_a=w_out_a[0].astype(_BF16), w_out_b=w_out_b[0].astype(_BF16), w_o=w_o[0].astype(_BF16),
        ln1_g=ln1_g[0].reshape(1, d), ln1_b=ln1_b[0].reshape(1, d),
        wr_hi=wr_hi, wr_lo=(wr - wr_hi.astype(_F32)).astype(_BF16),
        rbias=router_bias[0].reshape(N_EXPERTS, 1),
        tri=(jnp.arange(ts)[:, None] < jnp.arange(ts)[None, :]).astype(_BF16),
        w_gate=w_gate[0], w_up=w_up[0], w_down=w_down[0],
        ws_gate=ws_gate[0].astype(_BF16), ws_up=ws_up[0].astype(_BF16), ws_down=ws_down[0].astype(_BF16),
        ln2_g=ln2_g[0].reshape(1, d), ln2_b=ln2_b[0].reshape(1, d),
    )


def kernel(x_prompt, x_sample, w_in, conv_a_w, conv_a_b, w_out_a, conv_b_w, conv_b_b, ln_c_g, ln_c_b,
           w_out_b, w_o, ln1_g, ln1_b, w_router, router_bias, w_gate, w_up, w_down, ws_gate, ws_up,
           ws_down, ln2_g, ln2_b):
    p = _prepare_params(w_in, conv_a_w, conv_a_b, w_out_a, conv_b_w, conv_b_b, ln_c_g, ln_c_b,
                        w_out_b, w_o, ln1_g, ln1_b, w_router, router_bias, w_gate, w_up, w_down,
                        ws_gate, ws_up, ws_down, ln2_g, ln2_b)
    return (_trunk(x_prompt, p), _trunk(x_sample, p))
```

```python
import functools

import jax
import jax.numpy as jnp
from jax import lax
from jax.experimental import pallas as pl
from jax.experimental.pallas import tpu as pltpu
from jax.experimental.pallas import tpu_sc as plsc

K_SHORT = 3
K_CONF = 31
N_EXPERTS = 256
TOP_K = 8
N_GROUPS = 8
TOPK_GROUPS = 4
GROUP_SIZE = N_EXPERTS // N_GROUPS
ROUTE_SCALE = 2.5
LN_EPS = 1e-5
DEPTH = 1
ALPHA = (2.0 * DEPTH) ** 0.25
N_IN_PARTS = 7

V7X_LANES = 128
V7X_VMEM_LIMIT_BYTES = 56 * 1024 * 1024

HALO = 16
SEQ_TILE = 512
COL_CHUNK = 256
EXPERT_ROWS = 256
EXPERT_IN_BUFFERS = 4
TOKEN_TILE = 256
SLOT_TILE = 2048

V7X_SC_CORES = 2
V7X_SC_SUBCORES = 16
SC_CHUNK_ROWS = 64

_F32 = jnp.float32
_BF16 = jnp.bfloat16
_HIGH_HALF = -65536


def _dot(a, b):
    return jnp.dot(a, b, preferred_element_type=_F32)


def _dot_nt(a, b):
    return lax.dot_general(a, b, (((1,), (1,)), ((), ())), preferred_element_type=_F32)


def _sigmoid(v):
    return 1.0 / (1.0 + jnp.exp(-v))


def _pack_bf16_pairs(v):
    h = v.shape[1] // 2
    hi = lax.bitcast_convert_type(v[:, :h].astype(_BF16).astype(_F32), jnp.int32)
    lo = lax.bitcast_convert_type(v[:, h:].astype(_BF16).astype(_F32), jnp.int32)
    return (hi & _HIGH_HALF) | lax.shift_right_logical(lo, 16)


def _unpack_bf16_pairs(p):
    hi = lax.bitcast_convert_type(p & _HIGH_HALF, _F32)
    lo = lax.bitcast_convert_type(lax.shift_left(p, 16), _F32)
    return hi, lo


def _layer_norm_rows(v, g, b):
    mu = jnp.mean(v, axis=-1, keepdims=True)
    d = v - mu
    var = jnp.mean(d * d, axis=-1, keepdims=True)
    return d * lax.rsqrt(var + LN_EPS) * g + b


def _route(logits_t, rbias, ts):
    scores = _sigmoid(logits_t)
    biased = scores + rbias
    neg = -jnp.inf
    iota_g = lax.broadcasted_iota(jnp.int32, (GROUP_SIZE, ts), 0)
    gscore = []
    for g in range(N_GROUPS):
        v = biased[g * GROUP_SIZE:(g + 1) * GROUP_SIZE]
        m1 = jnp.max(v, axis=0, keepdims=True)
        i1 = jnp.min(jnp.where(v == m1, iota_g, GROUP_SIZE), axis=0, keepdims=True)
        m2 = jnp.max(jnp.where(iota_g == i1, neg, v), axis=0, keepdims=True)
        gscore.append(m1 + m2)
    masked = []
    for g in range(N_GROUPS):
        ahead = jnp.zeros((1, ts), jnp.int32)
        for h in range(N_GROUPS):
            if h == g:
                continue
            before = gscore[h] > gscore[g]
            if h < g:
                before = before | (gscore[h] == gscore[g])
            ahead = ahead + before.astype(jnp.int32)
        keep = ahead < TOPK_GROUPS
        masked.append(jnp.where(keep, biased[g * GROUP_SIZE:(g + 1) * GROUP_SIZE], neg))
    v = jnp.concatenate(masked, axis=0)

    iota_e = lax.broadcasted_iota(jnp.int32, (N_EXPERTS, ts), 0)
    ids, wts = [], []
    onehot = jnp.zeros((N_EXPERTS, ts), _F32)
    for _ in range(TOP_K):
        m = jnp.max(v, axis=0, keepdims=True)
        idx = jnp.min(jnp.where(v == m, iota_e, N_EXPERTS), axis=0, keepdims=True)
        hit = iota_e == idx
        wts.append(jnp.sum(jnp.where(hit, scores, 0.0), axis=0, keepdims=True))
        ids.append(idx)
        v = jnp.where(hit, neg, v)
        onehot = jnp.where(hit, 1.0, onehot)
    total = wts[0]
    for w in wts[1:]:
        total = total + w
    wts = [w / total * ROUTE_SCALE for w in wts]
    return ids, wts, onehot


def _mixer_kernel(xm_ref, xp_ref, xn_ref,
                  wh_ref, wb_ref, wc_ref, wv_ref, wg_ref, wga_ref, wgb_ref,
                  caw_ref, cab_ref, cbw_ref, cbb_ref, lncg_ref, lncb_ref,
                  woa_ref, wob_ref, wo_ref, ln1g_ref, ln1b_ref,
                  wrh_ref, wrl_ref, rbias_ref, tri_ref,
                  x1_ref, x1p_ref, eid_ref, wts_ref, rank_ref, cnt_ref,
                  xe_s, pa_s, z_s, ga_s, gb_s, ch_s, zin_s, base_s,
                  *, ts, cw, nc, tiles_per_seq):
    i = pl.program_id(0)
    c = pl.program_id(1)
    pos = lax.rem(i, tiles_per_seq)

    @pl.when((i == 0) & (c == 0))
    def _():
        base_s[...] = jnp.zeros_like(base_s)

    @pl.when(c == 0)
    def _():
        keep_prev = (pos > 0).astype(_F32)
        keep_next = (pos < tiles_per_seq - 1).astype(_F32)
        xe_s[0:HALO, :] = (xp_ref[...] * keep_prev).astype(_BF16)
        xe_s[HALO:HALO + ts, :] = xm_ref[...].astype(_BF16)
        xe_s[HALO + ts:, :] = (xn_ref[...] * keep_next).astype(_BF16)

    xe = xe_s[...]
    xc = xe_s[HALO:HALO + ts, :]

    ch_s[...] = _dot(xe, wh_ref[...]) * _dot(xe, wc_ref[...])
    conv_a = cab_ref[...]
    for j in range(K_SHORT):
        conv_a = conv_a + caw_ref[j:j + 1, :] * ch_s[pl.ds(HALO - K_SHORT // 2 + j, ts), :]
    pa_s[c] = (_dot(xc, wb_ref[...]) * conv_a).astype(_BF16)

    zin_s[...] = _dot(xe, wv_ref[...]) * _sigmoid(_dot(xe, wg_ref[...]))
    conv_b = cbb_ref[...]
    for j in range(K_CONF):
        conv_b = conv_b + cbw_ref[j:j + 1, :] * zin_s[pl.ds(HALO - K_CONF // 2 + j, ts), :]
    z_s[c] = conv_b

    ga_s[c] = _sigmoid(_dot(xc, wga_ref[...]))
    gb_s[c] = _sigmoid(_dot(xc, wgb_ref[...]))

    @pl.when(c == nc - 1)
    def _():
        d_model = nc * cw
        mu = sum(jnp.sum(z_s[k], axis=-1, keepdims=True) for k in range(nc)) / d_model
        var = sum(jnp.sum((z_s[k] - mu) ** 2, axis=-1, keepdims=True) for k in range(nc)) / d_model
        inv = lax.rsqrt(var + LN_EPS)
        y_a = jnp.zeros((ts, d_model), _F32)
        y_b = jnp.zeros((ts, d_model), _F32)
        for k in range(nc):
            cols = slice(k * cw, (k + 1) * cw)
            zn = (z_s[k] - mu) * inv * lncg_ref[:, cols] + lncb_ref[:, cols]
            sw = zn * _sigmoid(zn)
            y_b = y_b + _dot(sw.astype(_BF16), wob_ref[cols, :])
            y_a = y_a + _dot(pa_s[k], woa_ref[cols, :])
        mix = jnp.zeros((ts, d_model), _F32)
        for k in range(nc):
            cols = slice(k * cw, (k + 1) * cw)
            merged = ga_s[k] * y_a[:, cols] + gb_s[k] * y_b[:, cols]
            mix = mix + _dot(merged.astype(_BF16), wo_ref[cols, :])
        x1 = _layer_norm_rows(ALPHA * xm_ref[...] + mix, ln1g_ref[...], ln1b_ref[...])
        x1_ref[...] = x1
        x1p_ref[...] = _pack_bf16_pairs(x1)

        xh = x1.astype(_BF16)
        xl = (x1 - xh.astype(_F32)).astype(_BF16)
        logits_t = _dot_nt(wrh_ref[...], xh) + (_dot_nt(wrl_ref[...], xh) + _dot_nt(wrh_ref[...], xl))
        ids, wts, onehot = _route(logits_t, rbias_ref[...], ts)

        before = _dot(onehot.astype(_BF16), tri_ref[...]) + base_s[:, 0:1]
        iota_e = lax.broadcasted_iota(jnp.int32, (N_EXPERTS, ts), 0)
        for k in range(TOP_K):
            eid_ref[k:k + 1, :] = ids[k]
            wts_ref[k:k + 1, :] = wts[k]
            r = jnp.sum(jnp.where(iota_e == ids[k], before, 0.0), axis=0, keepdims=True)
            rank_ref[k:k + 1, :] = r.astype(jnp.int32)
        base_s[...] = base_s[...] + jnp.sum(onehot, axis=1, keepdims=True)
        cnt_ref[...] = base_s[...].astype(jnp.int32)


def _mixer_call(x2d, seq_len, w_in_bf, conv_a_w, conv_a_b, conv_b_w, conv_b_b, ln_c_g, ln_c_b,
                w_out_a_bf, w_out_b_bf, w_o_bf, ln1_g, ln1_b, wr_hi, wr_lo, rbias, tri):
    t, d = x2d.shape
    ts, cw = SEQ_TILE, COL_CHUNK
    nc = d // cw
    assert seq_len % ts == 0 and d % cw == 0 and ts % HALO == 0
    n_tiles = t // ts
    halo_blocks_per_tile = ts // HALO
    n_halo_blocks = t // HALO

    def part_spec(p):
        return pl.BlockSpec((d, cw), lambda i, c, p=p: (0, p * nc + c))

    def chunk_rows(rows):
        return pl.BlockSpec((rows, cw), lambda i, c: (0, c))

    def const(shape):
        return pl.BlockSpec(shape, lambda i, c: (0,) * len(shape))

    tok_rows = lambda rows: pl.BlockSpec((rows, ts), lambda i, c: (0, i))

    in_specs = [
        pl.BlockSpec((ts, d), lambda i, c: (i, 0)),
        pl.BlockSpec((HALO, d), lambda i, c: (jnp.maximum(i * halo_blocks_per_tile - 1, 0), 0)),
        pl.BlockSpec((HALO, d),
                     lambda i, c: (jnp.minimum((i + 1) * halo_blocks_per_tile, n_halo_blocks - 1), 0)),
    ] + [part_spec(p) for p in range(N_IN_PARTS)] + [
        chunk_rows(K_SHORT), chunk_rows(1), chunk_rows(K_CONF), chunk_rows(1),
        const((1, d)), const((1, d)),
        const((d, d)), const((d, d)), const((d, d)),
        const((1, d)), const((1, d)),
        const((N_EXPERTS, d)), const((N_EXPERTS, d)), const((N_EXPERTS, 1)),
        const((ts, ts)),
    ]
    out_shape = [
        jax.ShapeDtypeStruct((t, d), _F32),
        jax.ShapeDtypeStruct((t, d // 2), jnp.int32),
        jax.ShapeDtypeStruct((TOP_K, t), jnp.int32),
        jax.ShapeDtypeStruct((TOP_K, t), _F32),
        jax.ShapeDtypeStruct((TOP_K, t), jnp.int32),
        jax.ShapeDtypeStruct((N_EXPERTS, V7X_LANES), jnp.int32),
    ]
    out_specs = [
        pl.BlockSpec((ts, d), lambda i, c: (i, 0)),
        pl.BlockSpec((ts, d // 2), lambda i, c: (i, 0)),
        tok_rows(TOP_K), tok_rows(TOP_K), tok_rows(TOP_K),
        const((N_EXPERTS, V7X_LANES)),
    ]
    scratch = [
        pltpu.VMEM((ts + 2 * HALO, d), _BF16),
        pltpu.VMEM((nc, ts, cw), _BF16),
        pltpu.VMEM((nc, ts, cw), _F32),
        pltpu.VMEM((nc, ts, cw), _F32),
        pltpu.VMEM((nc, ts, cw), _F32),
        pltpu.VMEM((ts + 2 * HALO, cw), _F32),
        pltpu.VMEM((ts + 2 * HALO, cw), _F32),
        pltpu.VMEM((N_EXPERTS, V7X_LANES), _F32),
    ]
    kern = functools.partial(_mixer_kernel, ts=ts, cw=cw, nc=nc, tiles_per_seq=seq_len // ts)
    w_parts = [w_in_bf] * N_IN_PARTS
    return pl.pallas_call(
        kern,
        grid=(n_tiles, nc),
        in_specs=in_specs,
        out_specs=out_specs,
        out_shape=out_shape,
        scratch_shapes=scratch,
        compiler_params=pltpu.CompilerParams(
            dimension_semantics=("arbitrary", "arbitrary"),
            vmem_limit_bytes=V7X_VMEM_LIMIT_BYTES),
        name="mixer_router",
    )(x2d, x2d, x2d, *w_parts, conv_a_w, conv_a_b, conv_b_w, conv_b_b, ln_c_g, ln_c_b,
      w_out_a_bf, w_out_b_bf, w_o_bf, ln1_g, ln1_b, wr_hi, wr_lo, rbias, tri)


def _slot_kernel(pstart_ref, eid_ref, rank_ref, slot_ref):
    eid = eid_ref[...]

    def body(e, acc):
        return jnp.where(eid == e, pstart_ref[e], acc)

    slot_ref[...] = lax.fori_loop(0, N_EXPERTS, body, jnp.zeros_like(eid), unroll=8) + rank_ref[...]


def _slot_call(pstart, eid, rank):
    t = eid.shape[1]
    tl = min(SLOT_TILE, t)
    spec = pl.BlockSpec((TOP_K, tl), lambda i, *_: (0, i))
    return pl.pallas_call(
        _slot_kernel,
        grid_spec=pltpu.PrefetchScalarGridSpec(
            num_scalar_prefetch=1, grid=(t // tl,), in_specs=[spec, spec], out_specs=spec),
        out_shape=jax.ShapeDtypeStruct(eid.shape, jnp.int32),
        compiler_params=pltpu.CompilerParams(dimension_semantics=("arbitrary",)),
        name="slot_lookup",
    )(pstart, eid, rank)


def _sc_gather_rows(table, idx):
    n, = idx.shape
    d = table.shape[1]
    n_workers = V7X_SC_CORES * V7X_SC_SUBCORES
    ch = SC_CHUNK_ROWS
    per_worker = n // n_workers
    assert n % (n_workers * ch) == 0
    mesh = plsc.VectorSubcoreMesh(core_axis_name="c", subcore_axis_name="s")

    @functools.partial(
        pl.kernel, mesh=mesh,
        out_type=jax.ShapeDtypeStruct((n, d), table.dtype),
        scratch_types=[pltpu.VMEM((ch,), jnp.int32), pltpu.VMEM((ch, d), table.dtype),
                       pltpu.SemaphoreType.DMA],
        name="sc_gather_rows",
    )
    def gather(table_hbm, idx_hbm, out_hbm, idx_v, rows_v, sem):
        worker = lax.axis_index("s") * V7X_SC_CORES + lax.axis_index("c")
        base = worker * per_worker

        @pl.loop(0, per_worker // ch)
        def _(c):
            off = base + c * ch
            pltpu.sync_copy(idx_hbm.at[pl.ds(off, ch)], idx_v)
            pltpu.async_copy(table_hbm.at[idx_v], rows_v, sem).wait()
            pltpu.sync_copy(rows_v, out_hbm.at[pl.ds(off, ch)])

    return gather(table, idx)


def _sc_dispatch_rows(x, slot_flat, n_rows):
    t, d = x.shape
    n_workers = V7X_SC_CORES * V7X_SC_SUBCORES
    ch = SC_CHUNK_ROWS
    per_worker = t // n_workers
    assert t % (n_workers * ch) == 0 and slot_flat.shape == (TOP_K * t,)
    mesh = plsc.VectorSubcoreMesh(core_axis_name="c", subcore_axis_name="s")

    @functools.partial(
        pl.kernel, mesh=mesh,
        out_type=jax.ShapeDtypeStruct((n_rows, d), x.dtype),
        scratch_types=[pltpu.VMEM((ch,), jnp.int32)] * TOP_K + [
            pltpu.VMEM((ch, d), x.dtype), pltpu.SemaphoreType.DMA, pltpu.SemaphoreType.DMA],
        name="sc_dispatch_rows",
    )
    def dispatch(x_hbm, slot_hbm, out_hbm, *scratch):
        idx_vs, rows_v, idx_sem, row_sem = scratch[:TOP_K], scratch[TOP_K], scratch[TOP_K + 1], scratch[TOP_K + 2]
        worker = lax.axis_index("s") * V7X_SC_CORES + lax.axis_index("c")
        base = worker * per_worker

        @pl.loop(0, per_worker // ch)
        def _(c):
            off = base + c * ch
            idx_copies = [pltpu.async_copy(slot_hbm.at[pl.ds(k * t + off, ch)], idx_vs[k], idx_sem)
                          for k in range(TOP_K)]
            pltpu.sync_copy(x_hbm.at[pl.ds(off, ch)], rows_v)
            for cp in idx_copies:
                cp.wait()
            row_copies = [pltpu.async_copy(rows_v, out_hbm.at[idx_vs[k]], row_sem) for k in range(TOP_K)]
            for cp in row_copies:
                cp.wait()

    return dispatch(x, slot_flat)


def _experts_kernel(blk_e_ref, blk_valid_ref, first_ref, next_e_ref, n_used_ref,
                    xs_hbm, wg_hbm, wu_hbm, wd_hbm, ys_hbm,
                    xbuf, ybuf, wg_f, wu_f, wd_f, wg_b, wu_b, wd_b, in_sem, out_sem, w_sem, *, bm):
    n_used = n_used_ref[0]
    half = xbuf.shape[2]

    def x_copy(b, slot):
        return pltpu.make_async_copy(xs_hbm.at[pl.ds(b * bm, bm)], xbuf.at[slot], in_sem.at[slot])

    def y_copy(b, slot):
        return pltpu.make_async_copy(ybuf.at[slot], ys_hbm.at[pl.ds(b * bm, bm)], out_sem.at[slot])

    def w_copies(e, wslot):
        return [pltpu.make_async_copy(src.at[e], dst.at[wslot], w_sem.at[wslot])
                for src, dst in ((wg_hbm, wg_f), (wu_hbm, wu_f), (wd_hbm, wd_f))]

    n_in = xbuf.shape[0]
    for j in range(n_in - 1):
        @pl.when(j < n_used)
        def _():
            x_copy(j, j).start()
    for cp in w_copies(blk_e_ref[0], 0):
        cp.start()

    def block(b, w_next):
        slot = lax.rem(b, 2)
        in_slot = lax.rem(b, n_in)
        ahead = b + (n_in - 1)

        @pl.when(ahead < n_used)
        def _():
            x_copy(ahead, lax.rem(ahead, n_in)).start()

        is_first = first_ref[b] == 1

        @pl.when(is_first)
        def _():
            for cp in w_copies(blk_e_ref[b], w_next):
                cp.wait()
            wg_b[...] = wg_f[w_next].astype(_BF16)
            wu_b[...] = wu_f[w_next].astype(_BF16)
            wd_b[...] = wd_f[w_next].astype(_BF16)

            @pl.when(next_e_ref[b] >= 0)
            def _():
                for cp in w_copies(next_e_ref[b], 1 - w_next):
                    cp.start()

        x_copy(b, in_slot).wait()

        @pl.when(b >= 2)
        def _():
            y_copy(b - 2, slot).wait()

        row = lax.broadcasted_iota(jnp.int32, (bm, half), 0)
        x_hi, x_lo = _unpack_bf16_pairs(jnp.where(row < blk_valid_ref[b], xbuf[in_slot], 0))
        x_hi, x_lo = x_hi.astype(_BF16), x_lo.astype(_BF16)
        g = _dot(x_hi, wg_b[:half, :]) + _dot(x_lo, wg_b[half:, :])
        u = _dot(x_hi, wu_b[:half, :]) + _dot(x_lo, wu_b[half:, :])
        h = (g * _sigmoid(g) * u).astype(_BF16)
        ybuf[slot] = _pack_bf16_pairs(_dot(h, wd_b[...]))
        y_copy(b, slot).start()
        return jnp.where(is_first, 1 - w_next, w_next)

    lax.fori_loop(0, n_used, block, jnp.int32(0))

    @pl.when(n_used >= 2)
    def _():
        y_copy(n_used - 2, lax.rem(n_used, 2)).wait()

    y_copy(n_used - 1, lax.rem(n_used - 1, 2)).wait()


def _experts_call(blk_e, blk_valid, blk_first, blk_next_e, n_used, xs, w_gate, w_up, w_down):
    n_rows, dp = xs.shape
    bm = EXPERT_ROWS
    d, f = w_gate.shape[-2:]
    assert d == 2 * dp
    any_spec = pl.BlockSpec(memory_space=pl.ANY)
    grid_spec = pltpu.PrefetchScalarGridSpec(
        num_scalar_prefetch=5,
        grid=(1,),
        in_specs=[any_spec] * 4,
        out_specs=any_spec,
        scratch_shapes=[
            pltpu.VMEM((EXPERT_IN_BUFFERS, bm, dp), jnp.int32), pltpu.VMEM((2, bm, dp), jnp.int32),
            pltpu.VMEM((2, d, f), _F32), pltpu.VMEM((2, d, f), _F32), pltpu.VMEM((2, f, d), _F32),
            pltpu.VMEM((d, f), _BF16), pltpu.VMEM((d, f), _BF16), pltpu.VMEM((f, d), _BF16),
            pltpu.SemaphoreType.DMA((EXPERT_IN_BUFFERS,)), pltpu.SemaphoreType.DMA((2,)),
            pltpu.SemaphoreType.DMA((2,)),
        ],
    )
    return pl.pallas_call(
        functools.partial(_experts_kernel, bm=bm),
        grid_spec=grid_spec,
        out_shape=jax.ShapeDtypeStruct((n_rows, dp), jnp.int32),
        compiler_params=pltpu.CompilerParams(
            dimension_semantics=("arbitrary",),
            vmem_limit_bytes=V7X_VMEM_LIMIT_BYTES),
        name="expert_ffn",
    )(blk_e, blk_valid, blk_first, blk_next_e, n_used, xs, w_gate, w_up, w_down)


def _combine_kernel(wts_ref, x1_ref, rows_ref, wsg_ref, wsu_ref, wsd_ref, g_ref, b_ref, out_ref):
    x1 = x1_ref[...]
    xb = x1.astype(_BF16)
    g = _dot(xb, wsg_ref[...])
    u = _dot(xb, wsu_ref[...])
    acc = ALPHA * x1 + _dot((g * _sigmoid(g) * u).astype(_BF16), wsd_ref[...])
    half = rows_ref.shape[2]
    acc_hi, acc_lo = acc[:, :half], acc[:, half:]
    w = wts_ref[...]
    for k in range(TOP_K):
        r_hi, r_lo = _unpack_bf16_pairs(rows_ref[k])
        acc_hi = acc_hi + r_hi * w[:, k:k + 1]
        acc_lo = acc_lo + r_lo * w[:, k:k + 1]
    acc = jnp.concatenate([acc_hi, acc_lo], axis=1)
    out_ref[...] = _layer_norm_rows(acc, g_ref[...], b_ref[...])


def _combine_call(wts_tk, x1, rows, ws_gate_bf, ws_up_bf, ws_down_bf, ln2_g, ln2_b):
    t, d = x1.shape
    tt = TOKEN_TILE
    fs = ws_gate_bf.shape[-1]

    def const(shape):
        return pl.BlockSpec(shape, lambda i: (0,) * len(shape))

    return pl.pallas_call(
        _combine_kernel,
        grid=(t // tt,),
        in_specs=[
            pl.BlockSpec((tt, TOP_K), lambda i: (i, 0)),
            pl.BlockSpec((tt, d), lambda i: (i, 0)),
            pl.BlockSpec((TOP_K, tt, d // 2), lambda i: (0, i, 0)),
            const((d, fs)), const((d, fs)), const((fs, d)),
            const((1, d)), const((1, d)),
        ],
        out_specs=pl.BlockSpec((tt, d), lambda i: (i, 0)),
        out_shape=jax.ShapeDtypeStruct((t, d), _F32),
        compiler_params=pltpu.CompilerParams(
            dimension_semantics=("arbitrary",),
            vmem_limit_bytes=V7X_VMEM_LIMIT_BYTES),
        name="combine_shared_norm",
    )(wts_tk, x1, rows, ws_gate_bf, ws_up_bf, ws_down_bf, ln2_g, ln2_b)


def _trunk(x, p):
    bsz, seq, d = x.shape
    t = bsz * seq
    x1, x1p, eid, wts, rank, cnt = _mixer_call(
        x.reshape(t, d), seq, p["w_in"], p["conv_a_w"], p["conv_a_b"], p["conv_b_w"], p["conv_b_b"],
        p["ln_c_g"], p["ln_c_b"], p["w_out_a"], p["w_out_b"], p["w_o"], p["ln1_g"], p["ln1_b"],
        p["wr_hi"], p["wr_lo"], p["rbias"], p["tri"])

    bm = EXPERT_ROWS
    counts = cnt[:, 0]
    pcounts = (counts + bm - 1) // bm * bm
    pend = jnp.cumsum(pcounts)
    pstart = pend - pcounts
    n_blocks = (t * TOP_K + N_EXPERTS * (bm - 1)) // bm
    n_used = jnp.maximum(pend[-1] // bm, 1).astype(jnp.int32)
    blk_start = jnp.minimum(jnp.arange(n_blocks, dtype=jnp.int32), n_used - 1) * bm
    blk_e = jnp.minimum(jnp.sum(pend[None, :] <= blk_start[:, None], axis=1), N_EXPERTS - 1).astype(jnp.int32)
    blk_valid = jnp.clip(counts[blk_e] - (blk_start - pstart[blk_e]), 0, bm).astype(jnp.int32)
    blk_first = (blk_start == pstart[blk_e]).astype(jnp.int32)
    owner = jnp.where(counts > 0, jnp.arange(N_EXPERTS, dtype=jnp.int32), N_EXPERTS)
    next_owner = lax.cummin(jnp.concatenate([owner[1:], jnp.full((1,), N_EXPERTS, jnp.int32)]), reverse=True)
    blk_next_e = jnp.where(next_owner < N_EXPERTS, next_owner, -1)[blk_e].astype(jnp.int32)
    pstart = pstart.astype(jnp.int32)

    slot = _slot_call(pstart, eid, rank).reshape(TOP_K * t)
    xs = _sc_dispatch_rows(x1p, slot, n_blocks * bm)
    ys = _experts_call(blk_e, blk_valid, blk_first, blk_next_e, n_used.reshape(1), xs,
                       p["w_gate"], p["w_up"], p["w_down"])
    rows = _sc_gather_rows(ys, slot).reshape(TOP_K, t, d // 2)
    out = _combine_call(wts.T, x1, rows, p["ws_gate"], p["ws_up"], p["ws_down"], p["ln2_g"], p["ln2_b"])
    return out.reshape(bsz, seq, d)


def _prepare_params(w_in, conv_a_w, conv_a_b, w_out_a, conv_b_w, conv_b_b, ln_c_g, ln_c_b,
                    w_out_b, w_o, ln1_g, ln1_b, w_router, router_bias, w_gate, w_up, w_down,
                    ws_gate, ws_up, ws_down, ln2_g, ln2_b):
    assert w_in.shape[0] == DEPTH == 1
    d = w_in.shape[1]
    wr = w_router[0].T
    wr_hi = wr.astype(_BF16)
    ts = SEQ_TILE
    return dict(
        w_in=w_in[0].astype(_BF16),
        conv_a_w=conv_a_w[0], conv_a_b=conv_a_b[0].reshape(1, d),
        conv_b_w=conv_b_w[0], conv_b_b=conv_b_b[0].reshape(1, d),
        ln_c_g=ln_c_g[0].reshape(1, d), ln_c_b=ln_c_b[0].reshape(1, d),
        w_out_a=w_out_a[0].astype(_BF16), w_out_b=w_out_b[0].astype(_BF16), w_o=w_o[0].astype(_BF16),
        ln1_g=ln1_g[0].reshape(1, d), ln1_b=ln1_b[0].reshape(1, d),
        wr_hi=wr_hi, wr_lo=(wr - wr_hi.astype(_F32)).astype(_BF16),
        rbias=router_bias[0].reshape(N_EXPERTS, 1),
        tri=(jnp.arange(ts)[:, None] < jnp.arange(ts)[None, :]).astype(_BF16),
        w_gate=w_gate[0], w_up=w_up[0], w_down=w_down[0],
        ws_gate=ws_gate[0].astype(_BF16), ws_up=ws_up[0].astype(_BF16), ws_down=ws_down[0].astype(_BF16),
        ln2_g=ln2_g[0].reshape(1, d), ln2_b=ln2_b[0].reshape(1, d),
    )


def kernel(x_prompt, x_sample, w_in, conv_a_w, conv_a_b, w_out_a, conv_b_w, conv_b_b, ln_c_g, ln_c_b,
           w_out_b, w_o, ln1_g, ln1_b, w_router, router_bias, w_gate, w_up, w_down, ws_gate, ws_up,
           ws_down, ln2_g, ln2_b):
    p = _prepare_params(w_in, conv_a_w, conv_a_b, w_out_a, conv_b_w, conv_b_b, ln_c_g, ln_c_b,
                        w_out_b, w_o, ln1_g, ln1_b, w_router, router_bias, w_gate, w_up, w_down,
                        ws_gate, ws_up, ws_down, ln2_g, ln2_b)
    return (_trunk(x_prompt, p), _trunk(x_sample, p))
```

```python
import functools

import jax
import jax.numpy as jnp
from jax import lax
from jax.experimental import pallas as pl
from jax.experimental.pallas import tpu as pltpu
from jax.experimental.pallas import tpu_sc as plsc

K_SHORT = 3
K_CONF = 31
N_EXPERTS = 256
TOP_K = 8
N_GROUPS = 8
TOPK_GROUPS = 4
GROUP_SIZE = N_EXPERTS // N_GROUPS
ROUTE_SCALE = 2.5
LN_EPS = 1e-5
DEPTH = 1
ALPHA = (2.0 * DEPTH) ** 0.25
N_IN_PARTS = 7

V7X_LANES = 128
V7X_SUBLANES = 8
V7X_VMEM_LIMIT_BYTES = 56 * 1024 * 1024

HALO = 16
SEQ_TILE = 512
COL_CHUNK = 256
EXPERT_ROWS = 256
EXPERT_IN_BUFFERS = 4
TOKEN_TILE = 256
SLOT_TILE = 2048

V7X_SC_CORES = 2
V7X_SC_SUBCORES = 16
SC_CHUNK_ROWS = 64

_F32 = jnp.float32
_BF16 = jnp.bfloat16
_HIGH_HALF = -65536


def _dot(a, b):
    return jnp.dot(a, b, preferred_element_type=_F32)


def _dot_nt(a, b):
    return lax.dot_general(a, b, (((1,), (1,)), ((), ())), preferred_element_type=_F32)


def _sigmoid(v):
    return 1.0 / (1.0 + jnp.exp(-v))


def _pack_bf16_pairs(v):
    h = v.shape[1] // 2
    hi = lax.bitcast_convert_type(v[:, :h].astype(_BF16).astype(_F32), jnp.int32)
    lo = lax.bitcast_convert_type(v[:, h:].astype(_BF16).astype(_F32), jnp.int32)
    return (hi & _HIGH_HALF) | lax.shift_right_logical(lo, 16)


def _unpack_bf16_pairs(p):
    hi = lax.bitcast_convert_type(p & _HIGH_HALF, _F32)
    lo = lax.bitcast_convert_type(lax.shift_left(p, 16), _F32)
    return hi, lo


def _layer_norm_rows(v, g, b):
    mu = jnp.mean(v, axis=-1, keepdims=True)
    d = v - mu
    var = jnp.mean(d * d, axis=-1, keepdims=True)
    return d * lax.rsqrt(var + LN_EPS) * g + b


def _route(logits_t, rbias, ts):
    scores = _sigmoid(logits_t)
    biased = scores + rbias
    neg = -jnp.inf
    iota_g = lax.broadcasted_iota(jnp.int32, (GROUP_SIZE, ts), 0)
    gscore = []
    for g in range(N_GROUPS):
        v = biased[g * GROUP_SIZE:(g + 1) * GROUP_SIZE]
        m1 = jnp.max(v, axis=0, keepdims=True)
        i1 = jnp.min(jnp.where(v == m1, iota_g, GROUP_SIZE), axis=0, keepdims=True)
        m2 = jnp.max(jnp.where(iota_g == i1, neg, v), axis=0, keepdims=True)
        gscore.append(m1 + m2)
    masked = []
    for g in range(N_GROUPS):
        ahead = jnp.zeros((1, ts), jnp.int32)
        for h in range(N_GROUPS):
            if h == g:
                continue
            before = gscore[h] > gscore[g]
            if h < g:
                before = before | (gscore[h] == gscore[g])
            ahead = ahead + before.astype(jnp.int32)
        keep = ahead < TOPK_GROUPS
        masked.append(jnp.where(keep, biased[g * GROUP_SIZE:(g + 1) * GROUP_SIZE], neg))
    v = jnp.concatenate(masked, axis=0)

    iota_e = lax.broadcasted_iota(jnp.int32, (N_EXPERTS, ts), 0)
    ids, wts = [], []
    onehot = jnp.zeros((N_EXPERTS, ts), _F32)
    for _ in range(TOP_K):
        m = jnp.max(v, axis=0, keepdims=True)
        idx = jnp.min(jnp.where(v == m, iota_e, N_EXPERTS), axis=0, keepdims=True)
        hit = iota_e == idx
        wts.append(jnp.sum(jnp.where(hit, scores, 0.0), axis=0, keepdims=True))
        ids.append(idx)
        v = jnp.where(hit, neg, v)
        onehot = jnp.where(hit, 1.0, onehot)
    total = wts[0]
    for w in wts[1:]:
        total = total + w
    wts = [w / total * ROUTE_SCALE for w in wts]
    return ids, wts, onehot


def _mixer_kernel(xm_ref, xp_ref, xn_ref,
                  wh_ref, wb_ref, wc_ref, wv_ref, wg_ref, wga_ref, wgb_ref,
                  caw_ref, cab_ref, cbw_ref, cbb_ref, lncg_ref, lncb_ref,
                  woa_ref, wob_ref, wo_ref, ln1g_ref, ln1b_ref,
                  wrh_ref, wrl_ref, rbias_ref, tri_ref,
                  x1_ref, x1p_ref, eid_ref, wts_ref, rank_ref, cnt_ref,
                  xe_s, pa_s, z_s, ga_s, gb_s, ch_s, zsh_s, base_s,
                  *, ts, cw, nc, tiles_per_seq):
    i = pl.program_id(0)
    c = pl.program_id(1)
    pos = lax.rem(i, tiles_per_seq)

    @pl.when((i == 0) & (c == 0))
    def _():
        base_s[...] = jnp.zeros_like(base_s)

    @pl.when(c == 0)
    def _():
        keep_prev = (pos > 0).astype(_F32)
        keep_next = (pos < tiles_per_seq - 1).astype(_F32)
        xe_s[0:HALO, :] = (xp_ref[...] * keep_prev).astype(_BF16)
        xe_s[HALO:HALO + ts, :] = xm_ref[...].astype(_BF16)
        xe_s[HALO + ts:, :] = (xn_ref[...] * keep_next).astype(_BF16)

    xe = xe_s[...]
    xc = xe_s[HALO:HALO + ts, :]

    ch_s[...] = _dot(xe, wh_ref[...]) * _dot(xe, wc_ref[...])
    conv_a = cab_ref[...]
    for j in range(K_SHORT):
        conv_a = conv_a + caw_ref[j:j + 1, :] * ch_s[pl.ds(HALO - K_SHORT // 2 + j, ts), :]
    pa_s[c] = (_dot(xc, wb_ref[...]) * conv_a).astype(_BF16)

    zsh_s[0] = _dot(xe, wv_ref[...]) * _sigmoid(_dot(xe, wg_ref[...]))
    span = zsh_s.shape[1] - V7X_SUBLANES
    for r in range(1, V7X_SUBLANES):
        zsh_s[r, 0:span, :] = zsh_s[0, pl.ds(r, span), :]
    conv_b = cbb_ref[...]
    for j in range(K_CONF):
        first = HALO - K_CONF // 2 + j
        aligned = first // V7X_SUBLANES * V7X_SUBLANES
        conv_b = conv_b + cbw_ref[j:j + 1, :] * zsh_s[first - aligned, pl.ds(aligned, ts), :]
    z_s[c] = conv_b

    ga_s[c] = _sigmoid(_dot(xc, wga_ref[...]))
    gb_s[c] = _sigmoid(_dot(xc, wgb_ref[...]))

    @pl.when(c == nc - 1)
    def _():
        d_model = nc * cw
        mu = sum(jnp.sum(z_s[k], axis=-1, keepdims=True) for k in range(nc)) / d_model
        var = sum(jnp.sum((z_s[k] - mu) ** 2, axis=-1, keepdims=True) for k in range(nc)) / d_model
        inv = lax.rsqrt(var + LN_EPS)
        y_a = jnp.zeros((ts, d_model), _F32)
        y_b = jnp.zeros((ts, d_model), _F32)
        for k in range(nc):
            cols = slice(k * cw, (k + 1) * cw)
            zn = (z_s[k] - mu) * inv * lncg_ref[:, cols] + lncb_ref[:, cols]
            sw = zn * _sigmoid(zn)
            y_b = y_b + _dot(sw.astype(_BF16), wob_ref[cols, :])
            y_a = y_a + _dot(pa_s[k], woa_ref[cols, :])
        mix = jnp.zeros((ts, d_model), _F32)
        for k in range(nc):
            cols = slice(k * cw, (k + 1) * cw)
            merged = ga_s[k] * y_a[:, cols] + gb_s[k] * y_b[:, cols]
            mix = mix + _dot(merged.astype(_BF16), wo_ref[cols, :])
        x1 = _layer_norm_rows(ALPHA * xm_ref[...] + mix, ln1g_ref[...], ln1b_ref[...])
        x1_ref[...] = x1
        x1p_ref[...] = _pack_bf16_pairs(x1)

        xh = x1.astype(_BF16)
        xl = (x1 - xh.astype(_F32)).astype(_BF16)
        logits_t = _dot_nt(wrh_ref[...], xh) + (_dot_nt(wrl_ref[...], xh) + _dot_nt(wrh_ref[...], xl))
        ids, wts, onehot = _route(logits_t, rbias_ref[...], ts)

        before = _dot(onehot.astype(_BF16), tri_ref[...]) + base_s[:, 0:1]
        iota_e = lax.broadcasted_iota(jnp.int32, (N_EXPERTS, ts), 0)
        for k in range(TOP_K):
            eid_ref[k:k + 1, :] = ids[k]
            wts_ref[k:k + 1, :] = wts[k]
            r = jnp.sum(jnp.where(iota_e == ids[k], before, 0.0), axis=0, keepdims=True)
            rank_ref[k:k + 1, :] = r.astype(jnp.int32)
        base_s[...] = base_s[...] + jnp.sum(onehot, axis=1, keepdims=True)
        cnt_ref[...] = base_s[...].astype(jnp.int32)


def _mixer_call(x2d, seq_len, w_in_bf, conv_a_w, conv_a_b, conv_b_w, conv_b_b, ln_c_g, ln_c_b,
                w_out_a_bf, w_out_b_bf, w_o_bf, ln1_g, ln1_b, wr_hi, wr_lo, rbias, tri):
    t, d = x2d.shape
    ts, cw = SEQ_TILE, COL_CHUNK
    nc = d // cw
    assert seq_len % ts == 0 and d % cw == 0 and ts % HALO == 0
    n_tiles = t // ts
    halo_blocks_per_tile = ts // HALO
    n_halo_blocks = t // HALO

    def part_spec(p):
        return pl.BlockSpec((d, cw), lambda i, c, p=p: (0, p * nc + c))

    def chunk_rows(rows):
        return pl.BlockSpec((rows, cw), lambda i, c: (0, c))

    def const(shape):
        return pl.BlockSpec(shape, lambda i, c: (0,) * len(shape))

    tok_rows = lambda rows: pl.BlockSpec((rows, ts), lambda i, c: (0, i))

    in_specs = [
        pl.BlockSpec((ts, d), lambda i, c: (i, 0)),
        pl.BlockSpec((HALO, d), lambda i, c: (jnp.maximum(i * halo_blocks_per_tile - 1, 0), 0)),
        pl.BlockSpec((HALO, d),
                     lambda i, c: (jnp.minimum((i + 1) * halo_blocks_per_tile, n_halo_blocks - 1), 0)),
    ] + [part_spec(p) for p in range(N_IN_PARTS)] + [
        chunk_rows(K_SHORT), chunk_rows(1), chunk_rows(K_CONF), chunk_rows(1),
        const((1, d)), const((1, d)),
        const((d, d)), const((d, d)), const((d, d)),
        const((1, d)), const((1, d)),
        const((N_EXPERTS, d)), const((N_EXPERTS, d)), const((N_EXPERTS, 1)),
        const((ts, ts)),
    ]
    out_shape = [
        jax.ShapeDtypeStruct((t, d), _F32),
        jax.ShapeDtypeStruct((t, d // 2), jnp.int32),
        jax.ShapeDtypeStruct((TOP_K, t), jnp.int32),
        jax.ShapeDtypeStruct((TOP_K, t), _F32),
        jax.ShapeDtypeStruct((TOP_K, t), jnp.int32),
        jax.ShapeDtypeStruct((N_EXPERTS, V7X_LANES), jnp.int32),
    ]
    out_specs = [
        pl.BlockSpec((ts, d), lambda i, c: (i, 0)),
        pl.BlockSpec((ts, d // 2), lambda i, c: (i, 0)),
        tok_rows(TOP_K), tok_rows(TOP_K), tok_rows(TOP_K),
        const((N_EXPERTS, V7X_LANES)),
    ]
    scratch = [
        pltpu.VMEM((ts + 2 * HALO, d), _BF16),
        pltpu.VMEM((nc, ts, cw), _BF16),
        pltpu.VMEM((nc, ts, cw), _F32),
        pltpu.VMEM((nc, ts, cw), _F32),
        pltpu.VMEM((nc, ts, cw), _F32),
        pltpu.VMEM((ts + 2 * HALO, cw), _F32),
        pltpu.VMEM((V7X_SUBLANES, ts + 2 * HALO, cw), _F32),
        pltpu.VMEM((N_EXPERTS, V7X_LANES), _F32),
    ]
    kern = functools.partial(_mixer_kernel, ts=ts, cw=cw, nc=nc, tiles_per_seq=seq_len // ts)
    w_parts = [w_in_bf] * N_IN_PARTS
    return pl.pallas_call(
        kern,
        grid=(n_tiles, nc),
        in_specs=in_specs,
        out_specs=out_specs,
        out_shape=out_shape,
        scratch_shapes=scratch,
        compiler_params=pltpu.CompilerParams(
            dimension_semantics=("arbitrary", "arbitrary"),
            vmem_limit_bytes=V7X_VMEM_LIMIT_BYTES),
        name="mixer_router",
    )(x2d, x2d, x2d, *w_parts, conv_a_w, conv_a_b, conv_b_w, conv_b_b, ln_c_g, ln_c_b,
      w_out_a_bf, w_out_b_bf, w_o_bf, ln1_g, ln1_b, wr_hi, wr_lo, rbias, tri)


def _slot_kernel(pstart_ref, eid_ref, rank_ref, slot_ref):
    eid = eid_ref[...]

    def body(e, acc):
        return jnp.where(eid == e, pstart_ref[e], acc)

    slot_ref[...] = lax.fori_loop(0, N_EXPERTS, body, jnp.zeros_like(eid), unroll=8) + rank_ref[...]


def _slot_call(pstart, eid, rank):
    t = eid.shape[1]
    tl = min(SLOT_TILE, t)
    spec = pl.BlockSpec((TOP_K, tl), lambda i, *_: (0, i))
    return pl.pallas_call(
        _slot_kernel,
        grid_spec=pltpu.PrefetchScalarGridSpec(
            num_scalar_prefetch=1, grid=(t // tl,), in_specs=[spec, spec], out_specs=spec),
        out_shape=jax.ShapeDtypeStruct(eid.shape, jnp.int32),
        compiler_params=pltpu.CompilerParams(dimension_semantics=("arbitrary",)),
        name="slot_lookup",
    )(pstart, eid, rank)


def _sc_gather_rows(table, idx):
    n, = idx.shape
    d = table.shape[1]
    n_workers = V7X_SC_CORES * V7X_SC_SUBCORES
    ch = SC_CHUNK_ROWS
    per_worker = n // n_workers
    assert n % (n_workers * ch) == 0
    mesh = plsc.VectorSubcoreMesh(core_axis_name="c", subcore_axis_name="s")

    @functools.partial(
        pl.kernel, mesh=mesh,
        out_type=jax.ShapeDtypeStruct((n, d), table.dtype),
        scratch_types=[pltpu.VMEM((ch,), jnp.int32), pltpu.VMEM((ch, d), table.dtype),
                       pltpu.SemaphoreType.DMA],
        name="sc_gather_rows",
    )
    def gather(table_hbm, idx_hbm, out_hbm, idx_v, rows_v, sem):
        worker = lax.axis_index("s") * V7X_SC_CORES + lax.axis_index("c")
        base = worker * per_worker

        @pl.loop(0, per_worker // ch)
        def _(c):
            off = base + c * ch
            pltpu.sync_copy(idx_hbm.at[pl.ds(off, ch)], idx_v)
            pltpu.async_copy(table_hbm.at[idx_v], rows_v, sem).wait()
            pltpu.sync_copy(rows_v, out_hbm.at[pl.ds(off, ch)])

    return gather(table, idx)


def _sc_dispatch_rows(x, slot_flat, n_rows):
    t, d = x.shape
    n_workers = V7X_SC_CORES * V7X_SC_SUBCORES
    ch = SC_CHUNK_ROWS
    per_worker = t // n_workers
    assert t % (n_workers * ch) == 0 and slot_flat.shape == (TOP_K * t,)
    mesh = plsc.VectorSubcoreMesh(core_axis_name="c", subcore_axis_name="s")

    @functools.partial(
        pl.kernel, mesh=mesh,
        out_type=jax.ShapeDtypeStruct((n_rows, d), x.dtype),
        scratch_types=[pltpu.VMEM((ch,), jnp.int32)] * TOP_K + [
            pltpu.VMEM((ch, d), x.dtype), pltpu.SemaphoreType.DMA, pltpu.SemaphoreType.DMA],
        name="sc_dispatch_rows",
    )
    def dispatch(x_hbm, slot_hbm, out_hbm, *scratch):
        idx_vs, rows_v, idx_sem, row_sem = scratch[:TOP_K], scratch[TOP_K], scratch[TOP_K + 1], scratch[TOP_K + 2]
        worker = lax.axis_index("s") * V7X_SC_CORES + lax.axis_index("c")
        base = worker * per_worker

        @pl.loop(0, per_worker // ch)
        def _(c):
            off = base + c * ch
            idx_copies = [pltpu.async_copy(slot_hbm.at[pl.ds(k * t + off, ch)], idx_vs[k], idx_sem)
                          for k in range(TOP_K)]
            pltpu.sync_copy(x_hbm.at[pl.ds(off, ch)], rows_v)
            for cp in idx_copies:
                cp.wait()
            row_copies = [pltpu.async_copy(rows_v, out_hbm.at[idx_vs[k]], row_sem) for k in range(TOP_K)]
            for cp in row_copies:
                cp.wait()

    return dispatch(x, slot_flat)


def _experts_kernel(blk_e_ref, blk_valid_ref, first_ref, next_e_ref, n_used_ref,
                    xs_hbm, wg_hbm, wu_hbm, wd_hbm, ys_hbm,
                    xbuf, ybuf, wg_f, wu_f, wd_f, wg_b, wu_b, wd_b, in_sem, out_sem, w_sem, *, bm):
    n_used = n_used_ref[0]
    half = xbuf.shape[2]

    def x_copy(b, slot):
        return pltpu.make_async_copy(xs_hbm.at[pl.ds(b * bm, bm)], xbuf.at[slot], in_sem.at[slot])

    def y_copy(b, slot):
        return pltpu.make_async_copy(ybuf.at[slot], ys_hbm.at[pl.ds(b * bm, bm)], out_sem.at[slot])

    def w_copies(e, wslot):
        return [pltpu.make_async_copy(src.at[e], dst.at[wslot], w_sem.at[wslot])
                for src, dst in ((wg_hbm, wg_f), (wu_hbm, wu_f), (wd_hbm, wd_f))]

    n_in = xbuf.shape[0]
    for j in range(n_in - 1):
        @pl.when(j < n_used)
        def _():
            x_copy(j, j).start()
    for cp in w_copies(blk_e_ref[0], 0):
        cp.start()

    def block(b, w_next):
        slot = lax.rem(b, 2)
        in_slot = lax.rem(b, n_in)
        ahead = b + (n_in - 1)

        @pl.when(ahead < n_used)
        def _():
            x_copy(ahead, lax.rem(ahead, n_in)).start()

        is_first = first_ref[b] == 1

        @pl.when(is_first)
        def _():
            for cp in w_copies(blk_e_ref[b], w_next):
                cp.wait()
            wg_b[...] = wg_f[w_next].astype(_BF16)
            wu_b[...] = wu_f[w_next].astype(_BF16)
            wd_b[...] = wd_f[w_next].astype(_BF16)

            @pl.when(next_e_ref[b] >= 0)
            def _():
                for cp in w_copies(next_e_ref[b], 1 - w_next):
                    cp.start()

        x_copy(b, in_slot).wait()

        @pl.when(b >= 2)
        def _():
            y_copy(b - 2, slot).wait()

        row = lax.broadcasted_iota(jnp.int32, (bm, half), 0)
        x_hi, x_lo = _unpack_bf16_pairs(jnp.where(row < blk_valid_ref[b], xbuf[in_slot], 0))
        x_hi, x_lo = x_hi.astype(_BF16), x_lo.astype(_BF16)
        g = _dot(x_hi, wg_b[:half, :]) + _dot(x_lo, wg_b[half:, :])
        u = _dot(x_hi, wu_b[:half, :]) + _dot(x_lo, wu_b[half:, :])
        h = (g * _sigmoid(g) * u).astype(_BF16)
        ybuf[slot] = _pack_bf16_pairs(_dot(h, wd_b[...]))
        y_copy(b, slot).start()
        return jnp.where(is_first, 1 - w_next, w_next)

    lax.fori_loop(0, n_used, block, jnp.int32(0))

    @pl.when(n_used >= 2)
    def _():
        y_copy(n_used - 2, lax.rem(n_used, 2)).wait()

    y_copy(n_used - 1, lax.rem(n_used - 1, 2)).wait()


def _experts_call(blk_e, blk_valid, blk_first, blk_next_e, n_used, xs, w_gate, w_up, w_down):
    n_rows, dp = xs.shape
    bm = EXPERT_ROWS
    d, f = w_gate.shape[-2:]
    assert d == 2 * dp
    any_spec = pl.BlockSpec(memory_space=pl.ANY)
    grid_spec = pltpu.PrefetchScalarGridSpec(
        num_scalar_prefetch=5,
        grid=(1,),
        in_specs=[any_spec] * 4,
        out_specs=any_spec,
        scratch_shapes=[
            pltpu.VMEM((EXPERT_IN_BUFFERS, bm, dp), jnp.int32), pltpu.VMEM((2, bm, dp), jnp.int32),
            pltpu.VMEM((2, d, f), _F32), pltpu.VMEM((2, d, f), _F32), pltpu.VMEM((2, f, d), _F32),
            pltpu.VMEM((d, f), _BF16), pltpu.VMEM((d, f), _BF16), pltpu.VMEM((f, d), _BF16),
            pltpu.SemaphoreType.DMA((EXPERT_IN_BUFFERS,)), pltpu.SemaphoreType.DMA((2,)),
            pltpu.SemaphoreType.DMA((2,)),
        ],
    )
    return pl.pallas_call(
        functools.partial(_experts_kernel, bm=bm),
        grid_spec=grid_spec,
        out_shape=jax.ShapeDtypeStruct((n_rows, dp), jnp.int32),
        compiler_params=pltpu.CompilerParams(
            dimension_semantics=("arbitrary",),
            vmem_limit_bytes=V7X_VMEM_LIMIT_BYTES),
        name="expert_ffn",
    )(blk_e, blk_valid, blk_first, blk_next_e, n_used, xs, w_gate, w_up, w_down)


def _combine_kernel(wts_ref, x1_ref, rows_ref, wsg_ref, wsu_ref, wsd_ref, g_ref, b_ref, out_ref):
    x1 = x1_ref[...]
    xb = x1.astype(_BF16)
    g = _dot(xb, wsg_ref[...])
    u = _dot(xb, wsu_ref[...])
    acc = ALPHA * x1 + _dot((g * _sigmoid(g) * u).astype(_BF16), wsd_ref[...])
    half = rows_ref.shape[2]
    acc_hi, acc_lo = acc[:, :half], acc[:, half:]
    w = wts_ref[...]
    for k in range(TOP_K):
        r_hi, r_lo = _unpack_bf16_pairs(rows_ref[k])
        acc_hi = acc_hi + r_hi * w[:, k:k + 1]
        acc_lo = acc_lo + r_lo * w[:, k:k + 1]
    acc = jnp.concatenate([acc_hi, acc_lo], axis=1)
    out_ref[...] = _layer_norm_rows(acc, g_ref[...], b_ref[...])


def _combine_call(wts_tk, x1, rows, ws_gate_bf, ws_up_bf, ws_down_bf, ln2_g, ln2_b):
    t, d = x1.shape
    tt = TOKEN_TILE
    fs = ws_gate_bf.shape[-1]

    def const(shape):
        return pl.BlockSpec(shape, lambda i: (0,) * len(shape))

    return pl.pallas_call(
        _combine_kernel,
        grid=(t // tt,),
        in_specs=[
            pl.BlockSpec((tt, TOP_K), lambda i: (i, 0)),
            pl.BlockSpec((tt, d), lambda i: (i, 0)),
            pl.BlockSpec((TOP_K, tt, d // 2), lambda i: (0, i, 0)),
            const((d, fs)), const((d, fs)), const((fs, d)),
            const((1, d)), const((1, d)),
        ],
        out_specs=pl.BlockSpec((tt, d), lambda i: (i, 0)),
        out_shape=jax.ShapeDtypeStruct((t, d), _F32),
        compiler_params=pltpu.CompilerParams(
            dimension_semantics=("arbitrary",),
            vmem_limit_bytes=V7X_VMEM_LIMIT_BYTES),
        name="combine_shared_norm",
    )(wts_tk, x1, rows, ws_gate_bf, ws_up_bf, ws_down_bf, ln2_g, ln2_b)


def _trunk(x, p):
    bsz, seq, d = x.shape
    t = bsz * seq
    x1, x1p, eid, wts, rank, cnt = _mixer_call(
        x.reshape(t, d), seq, p["w_in"], p["conv_a_w"], p["conv_a_b"], p["conv_b_w"], p["conv_b_b"],
        p["ln_c_g"], p["ln_c_b"], p["w_out_a"], p["w_out_b"], p["w_o"], p["ln1_g"], p["ln1_b"],
        p["wr_hi"], p["wr_lo"], p["rbias"], p["tri"])

    bm = EXPERT_ROWS
    counts = cnt[:, 0]
    pcounts = (counts + bm - 1) // bm * bm
    pend = jnp.cumsum(pcounts)
    pstart = pend - pcounts
    n_blocks = (t * TOP_K + N_EXPERTS * (bm - 1)) // bm
    n_used = jnp.maximum(pend[-1] // bm, 1).astype(jnp.int32)
    blk_start = jnp.minimum(jnp.arange(n_blocks, dtype=jnp.int32), n_used - 1) * bm
    blk_e = jnp.minimum(jnp.sum(pend[None, :] <= blk_start[:, None], axis=1), N_EXPERTS - 1).astype(jnp.int32)
    blk_valid = jnp.clip(counts[blk_e] - (blk_start - pstart[blk_e]), 0, bm).astype(jnp.int32)
    blk_first = (blk_start == pstart[blk_e]).astype(jnp.int32)
    owner = jnp.where(counts > 0, jnp.arange(N_EXPERTS, dtype=jnp.int32), N_EXPERTS)
    next_owner = lax.cummin(jnp.concatenate([owner[1:], jnp.full((1,), N_EXPERTS, jnp.int32)]), reverse=True)
    blk_next_e = jnp.where(next_owner < N_EXPERTS, next_owner, -1)[blk_e].astype(jnp.int32)
    pstart = pstart.astype(jnp.int32)

    slot = _slot_call(pstart, eid, rank).reshape(TOP_K * t)
    xs = _sc_dispatch_rows(x1p, slot, n_blocks * bm)
    ys = _experts_call(blk_e, blk_valid, blk_first, blk_next_e, n_used.reshape(1), xs,
                       p["w_gate"], p["w_up"], p["w_down"])
    rows = _sc_gather_rows(ys, slot).reshape(TOP_K, t, d // 2)
    out = _combine_call(wts.T, x1, rows, p["ws_gate"], p["ws_up"], p["ws_down"], p["ln2_g"], p["ln2_b"])
    return out.reshape(bsz, seq, d)


def _prepare_params(w_in, conv_a_w, conv_a_b, w_out_a, conv_b_w, conv_b_b, ln_c_g, ln_c_b,
                    w_out_b, w_o, ln1_g, ln1_b, w_router, router_bias, w_gate, w_up, w_down,
                    ws_gate, ws_up, ws_down, ln2_g, ln2_b):
    assert w_in.shape[0] == DEPTH == 1
    d = w_in.shape[1]
    wr = w_router[0].T
    wr_hi = wr.astype(_BF16)
    ts = SEQ_TILE
    return dict(
        w_in=w_in[0].astype(_BF16),
        conv_a_w=conv_a_w[0], conv_a_b=conv_a_b[0].reshape(1, d),
        conv_b_w=conv_b_w[0], conv_b_b=conv_b_b[0].reshape(1, d),
        ln_c_g=ln_c_g[0].reshape(1, d), ln_c_b=ln_c_b[0].reshape(1, d),
        w_out_a=w_out_a[0].astype(_BF16), w_out_b=w_out_b[0].astype(_BF16), w_o=w_o[0].astype(_BF16),
        ln1_g=ln1_g[0].reshape(1, d), ln1_b=ln1_b[0].reshape(1, d),
        wr_hi=wr_hi, wr_lo=(wr - wr_hi.astype(_F32)).astype(_BF16),
        rbias=router_bias[0].reshape(N_EXPERTS, 1),
        tri=(jnp.arange(ts)[:, None] < jnp.arange(ts)[None, :]).astype(_BF16),
        w_gate=w_gate[0], w_up=w_up[0], w_down=w_down[0],
        ws_gate=ws_gate[0].astype(_BF16), ws_up=ws_up[0].astype(_BF16), ws_down=ws_down[0].astype(_BF16),
        ln2_g=ln2_g[0].reshape(1, d), ln2_b=ln2_b[0].reshape(1, d),
    )


def kernel(x_prompt, x_sample, w_in, conv_a_w, conv_a_b, w_out_a, conv_b_w, conv_b_b, ln_c_g, ln_c_b,
           w_out_b, w_o, ln1_g, ln1_b, w_router, router_bias, w_gate, w_up, w_down, ws_gate, ws_up,
           ws_down, ln2_g, ln2_b):
    p = _prepare_params(w_in, conv_a_w, conv_a_b, w_out_a, conv_b_w, conv_b_b, ln_c_g, ln_c_b,
                        w_out_b, w_o, ln1_g, ln1_b, w_router, router_bias, w_gate, w_up, w_down,
                        ws_gate, ws_up, ws_down, ln2_g, ln2_b)
    return (_trunk(x_prompt, p), _trunk(x_sample, p))
```

```python
import functools

import jax
import jax.numpy as jnp
from jax import lax
from jax.experimental import pallas as pl
from jax.experimental.pallas import tpu as pltpu
from jax.experimental.pallas import tpu_sc as plsc

K_SHORT = 3
K_CONF = 31
N_EXPERTS = 256
TOP_K = 8
N_GROUPS = 8
TOPK_GROUPS = 4
GROUP_SIZE = N_EXPERTS // N_GROUPS
ROUTE_SCALE = 2.5
LN_EPS = 1e-5
DEPTH = 1
ALPHA = (2.0 * DEPTH) ** 0.25
N_IN_PARTS = 7

V7X_LANES = 128
V7X_SUBLANES = 8
V7X_VMEM_LIMIT_BYTES = 56 * 1024 * 1024

HALO = 16
SEQ_TILE = 512
COL_CHUNK = 256
EXPERT_ROWS = 256
EXPERT_IN_BUFFERS = 6
EXPERT_OUT_BUFFERS = 3
TOKEN_TILE = 512
SLOT_TILE = 2048

V7X_SC_CORES = 2
V7X_SC_SUBCORES = 16
SC_CHUNK_ROWS = 64

_F32 = jnp.float32
_BF16 = jnp.bfloat16
_HIGH_HALF = -65536


def _dot(a, b):
    return jnp.dot(a, b, preferred_element_type=_F32)


def _dot_nt(a, b):
    return lax.dot_general(a, b, (((1,), (1,)), ((), ())), preferred_element_type=_F32)


def _sigmoid(v):
    return 1.0 / (1.0 + jnp.exp(-v))


def _pack_bf16_pairs(v):
    h = v.shape[1] // 2
    hi = lax.bitcast_convert_type(v[:, :h].astype(_BF16).astype(_F32), jnp.int32)
    lo = lax.bitcast_convert_type(v[:, h:].astype(_BF16).astype(_F32), jnp.int32)
    return (hi & _HIGH_HALF) | lax.shift_right_logical(lo, 16)


def _unpack_bf16_pairs(p):
    hi = lax.bitcast_convert_type(p & _HIGH_HALF, _F32)
    lo = lax.bitcast_convert_type(lax.shift_left(p, 16), _F32)
    return hi, lo


def _layer_norm_rows(v, g, b):
    mu = jnp.mean(v, axis=-1, keepdims=True)
    d = v - mu
    var = jnp.mean(d * d, axis=-1, keepdims=True)
    return d * lax.rsqrt(var + LN_EPS) * g + b


def _route(logits_t, rbias, ts):
    scores = _sigmoid(logits_t)
    biased = scores + rbias
    neg = -jnp.inf
    iota_g = lax.broadcasted_iota(jnp.int32, (GROUP_SIZE, ts), 0)
    gscore = []
    for g in range(N_GROUPS):
        v = biased[g * GROUP_SIZE:(g + 1) * GROUP_SIZE]
        m1 = jnp.max(v, axis=0, keepdims=True)
        i1 = jnp.min(jnp.where(v == m1, iota_g, GROUP_SIZE), axis=0, keepdims=True)
        m2 = jnp.max(jnp.where(iota_g == i1, neg, v), axis=0, keepdims=True)
        gscore.append(m1 + m2)
    masked = []
    for g in range(N_GROUPS):
        ahead = jnp.zeros((1, ts), jnp.int32)
        for h in range(N_GROUPS):
            if h == g:
                continue
            before = gscore[h] > gscore[g]
            if h < g:
                before = before | (gscore[h] == gscore[g])
            ahead = ahead + before.astype(jnp.int32)
        keep = ahead < TOPK_GROUPS
        masked.append(jnp.where(keep, biased[g * GROUP_SIZE:(g + 1) * GROUP_SIZE], neg))
    v = jnp.concatenate(masked, axis=0)

    iota_e = lax.broadcasted_iota(jnp.int32, (N_EXPERTS, ts), 0)
    ids, wts = [], []
    onehot = jnp.zeros((N_EXPERTS, ts), _F32)
    for _ in range(TOP_K):
        m = jnp.max(v, axis=0, keepdims=True)
        idx = jnp.min(jnp.where(v == m, iota_e, N_EXPERTS), axis=0, keepdims=True)
        hit = iota_e == idx
        wts.append(jnp.sum(jnp.where(hit, scores, 0.0), axis=0, keepdims=True))
        ids.append(idx)
        v = jnp.where(hit, neg, v)
        onehot = jnp.where(hit, 1.0, onehot)
    total = wts[0]
    for w in wts[1:]:
        total = total + w
    wts = [w / total * ROUTE_SCALE for w in wts]
    return ids, wts, onehot


def _mixer_kernel(xm_ref, xp_ref, xn_ref,
                  wh_ref, wb_ref, wc_ref, wv_ref, wg_ref, wga_ref, wgb_ref,
                  caw_ref, cab_ref, cbw_ref, cbb_ref, lncg_ref, lncb_ref,
                  woa_ref, wob_ref, wo_ref, ln1g_ref, ln1b_ref,
                  wrh_ref, wrl_ref, rbias_ref, tri_ref,
                  x1_ref, x1p_ref, eid_ref, wts_ref, rank_ref, cnt_ref,
                  xe_s, pa_s, z_s, ga_s, gb_s, ch_s, zsh_s, base_s,
                  *, ts, cw, nc, tiles_per_seq):
    i = pl.program_id(0)
    c = pl.program_id(1)
    pos = lax.rem(i, tiles_per_seq)

    @pl.when((i == 0) & (c == 0))
    def _():
        base_s[...] = jnp.zeros_like(base_s)

    @pl.when(c == 0)
    def _():
        keep_prev = (pos > 0).astype(_F32)
        keep_next = (pos < tiles_per_seq - 1).astype(_F32)
        xe_s[0:HALO, :] = (xp_ref[...] * keep_prev).astype(_BF16)
        xe_s[HALO:HALO + ts, :] = xm_ref[...].astype(_BF16)
        xe_s[HALO + ts:, :] = (xn_ref[...] * keep_next).astype(_BF16)

    xe = xe_s[...]
    xc = xe_s[HALO:HALO + ts, :]

    zsh_s[0] = _dot(xe, wv_ref[...]) * _sigmoid(_dot(xe, wg_ref[...]))
    span = zsh_s.shape[1] - V7X_SUBLANES
    for r in range(1, V7X_SUBLANES):
        zsh_s[r, 0:span, :] = zsh_s[0, pl.ds(r, span), :]

    ch_s[...] = _dot(xe, wh_ref[...]) * _dot(xe, wc_ref[...])
    u_b = _dot(xc, wb_ref[...])
    ga_s[c] = _sigmoid(_dot(xc, wga_ref[...]))
    gb_s[c] = _sigmoid(_dot(xc, wgb_ref[...]))

    conv_b = cbb_ref[...]
    for j in range(K_CONF):
        first = HALO - K_CONF // 2 + j
        aligned = first // V7X_SUBLANES * V7X_SUBLANES
        conv_b = conv_b + cbw_ref[j:j + 1, :] * zsh_s[first - aligned, pl.ds(aligned, ts), :]
    z_s[c] = conv_b

    conv_a = cab_ref[...]
    for j in range(K_SHORT):
        conv_a = conv_a + caw_ref[j:j + 1, :] * ch_s[pl.ds(HALO - K_SHORT // 2 + j, ts), :]
    pa_s[c] = (u_b * conv_a).astype(_BF16)

    @pl.when(c == nc - 1)
    def _():
        d_model = nc * cw
        mu = sum(jnp.sum(z_s[k], axis=-1, keepdims=True) for k in range(nc)) / d_model
        var = sum(jnp.sum((z_s[k] - mu) ** 2, axis=-1, keepdims=True) for k in range(nc)) / d_model
        inv = lax.rsqrt(var + LN_EPS)
        y_a = jnp.zeros((ts, d_model), _F32)
        y_b = jnp.zeros((ts, d_model), _F32)
        for k in range(nc):
            cols = slice(k * cw, (k + 1) * cw)
            zn = (z_s[k] - mu) * inv * lncg_ref[:, cols] + lncb_ref[:, cols]
            sw = zn * _sigmoid(zn)
            y_b = y_b + _dot(sw.astype(_BF16), wob_ref[cols, :])
            y_a = y_a + _dot(pa_s[k], woa_ref[cols, :])
        mix = jnp.zeros((ts, d_model), _F32)
        for k in range(nc):
            cols = slice(k * cw, (k + 1) * cw)
            merged = ga_s[k] * y_a[:, cols] + gb_s[k] * y_b[:, cols]
            mix = mix + _dot(merged.astype(_BF16), wo_ref[cols, :])
        x1 = _layer_norm_rows(ALPHA * xm_ref[...] + mix, ln1g_ref[...], ln1b_ref[...])
        x1_ref[...] = x1
        x1p_ref[...] = _pack_bf16_pairs(x1)

        xh = x1.astype(_BF16)
        xl = (x1 - xh.astype(_F32)).astype(_BF16)
        logits_t = _dot_nt(wrh_ref[...], xh) + (_dot_nt(wrl_ref[...], xh) + _dot_nt(wrh_ref[...], xl))
        ids, wts, onehot = _route(logits_t, rbias_ref[...], ts)

        before = _dot(onehot.astype(_BF16), tri_ref[...]) + base_s[:, 0:1]
        iota_e = lax.broadcasted_iota(jnp.int32, (N_EXPERTS, ts), 0)
        for k in range(TOP_K):
            eid_ref[k:k + 1, :] = ids[k]
            wts_ref[k:k + 1, :] = wts[k]
            r = jnp.sum(jnp.where(iota_e == ids[k], before, 0.0), axis=0, keepdims=True)
            rank_ref[k:k + 1, :] = r.astype(jnp.int32)
        base_s[...] = base_s[...] + jnp.sum(onehot, axis=1, keepdims=True)
        cnt_ref[...] = base_s[...].astype(jnp.int32)


def _mixer_call(x2d, seq_len, w_in_bf, conv_a_w, conv_a_b, conv_b_w, conv_b_b, ln_c_g, ln_c_b,
                w_out_a_bf, w_out_b_bf, w_o_bf, ln1_g, ln1_b, wr_hi, wr_lo, rbias, tri):
    t, d = x2d.shape
    ts, cw = SEQ_TILE, COL_CHUNK
    nc = d // cw
    assert seq_len % ts == 0 and d % cw == 0 and ts % HALO == 0
    n_tiles = t // ts
    halo_blocks_per_tile = ts // HALO
    n_halo_blocks = t // HALO

    def part_spec(p):
        return pl.BlockSpec((d, cw), lambda i, c, p=p: (0, p * nc + c))

    def chunk_rows(rows):
        return pl.BlockSpec((rows, cw), lambda i, c: (0, c))

    def const(shape):
        return pl.BlockSpec(shape, lambda i, c: (0,) * len(shape))

    tok_rows = lambda rows: pl.BlockSpec((rows, ts), lambda i, c: (0, i))

    in_specs = [
        pl.BlockSpec((ts, d), lambda i, c: (i, 0)),
        pl.BlockSpec((HALO, d), lambda i, c: (jnp.maximum(i * halo_blocks_per_tile - 1, 0), 0)),
        pl.BlockSpec((HALO, d),
                     lambda i, c: (jnp.minimum((i + 1) * halo_blocks_per_tile, n_halo_blocks - 1), 0)),
    ] + [part_spec(p) for p in range(N_IN_PARTS)] + [
        chunk_rows(K_SHORT), chunk_rows(1), chunk_rows(K_CONF), chunk_rows(1),
        const((1, d)), const((1, d)),
        const((d, d)), const((d, d)), const((d, d)),
        const((1, d)), const((1, d)),
        const((N_EXPERTS, d)), const((N_EXPERTS, d)), const((N_EXPERTS, 1)),
        const((ts, ts)),
    ]
    out_shape = [
        jax.ShapeDtypeStruct((t, d), _F32),
        jax.ShapeDtypeStruct((t, d // 2), jnp.int32),
        jax.ShapeDtypeStruct((TOP_K, t), jnp.int32),
        jax.ShapeDtypeStruct((TOP_K, t), _F32),
        jax.ShapeDtypeStruct((TOP_K, t), jnp.int32),
        jax.ShapeDtypeStruct((N_EXPERTS, V7X_LANES), jnp.int32),
    ]
    out_specs = [
        pl.BlockSpec((ts, d), lambda i, c: (i, 0)),
        pl.BlockSpec((ts, d // 2), lambda i, c: (i, 0)),
        tok_rows(TOP_K), tok_rows(TOP_K), tok_rows(TOP_K),
        const((N_EXPERTS, V7X_LANES)),
    ]
    scratch = [
        pltpu.VMEM((ts + 2 * HALO, d), _BF16),
        pltpu.VMEM((nc, ts, cw), _BF16),
        pltpu.VMEM((nc, ts, cw), _F32),
        pltpu.VMEM((nc, ts, cw), _F32),
        pltpu.VMEM((nc, ts, cw), _F32),
        pltpu.VMEM((ts + 2 * HALO, cw), _F32),
        pltpu.VMEM((V7X_SUBLANES, ts + 2 * HALO, cw), _F32),
        pltpu.VMEM((N_EXPERTS, V7X_LANES), _F32),
    ]
    kern = functools.partial(_mixer_kernel, ts=ts, cw=cw, nc=nc, tiles_per_seq=seq_len // ts)
    w_parts = [w_in_bf] * N_IN_PARTS
    return pl.pallas_call(
        kern,
        grid=(n_tiles, nc),
        in_specs=in_specs,
        out_specs=out_specs,
        out_shape=out_shape,
        scratch_shapes=scratch,
        compiler_params=pltpu.CompilerParams(
            dimension_semantics=("arbitrary", "arbitrary"),
            vmem_limit_bytes=V7X_VMEM_LIMIT_BYTES),
        name="mixer_router",
    )(x2d, x2d, x2d, *w_parts, conv_a_w, conv_a_b, conv_b_w, conv_b_b, ln_c_g, ln_c_b,
      w_out_a_bf, w_out_b_bf, w_o_bf, ln1_g, ln1_b, wr_hi, wr_lo, rbias, tri)


def _slot_kernel(pstart_ref, eid_ref, rank_ref, slot_ref):
    eid = eid_ref[...]

    def body(e, acc):
        return jnp.where(eid == e, pstart_ref[e], acc)

    slot_ref[...] = lax.fori_loop(0, N_EXPERTS, body, jnp.zeros_like(eid), unroll=8) + rank_ref[...]


def _slot_call(pstart, eid, rank):
    t = eid.shape[1]
    tl = min(SLOT_TILE, t)
    spec = pl.BlockSpec((TOP_K, tl), lambda i, *_: (0, i))
    return pl.pallas_call(
        _slot_kernel,
        grid_spec=pltpu.PrefetchScalarGridSpec(
            num_scalar_prefetch=1, grid=(t // tl,), in_specs=[spec, spec], out_specs=spec),
        out_shape=jax.ShapeDtypeStruct(eid.shape, jnp.int32),
        compiler_params=pltpu.CompilerParams(dimension_semantics=("arbitrary",)),
        name="slot_lookup",
    )(pstart, eid, rank)


def _sc_gather_rows(table, idx):
    n, = idx.shape
    d = table.shape[1]
    n_workers = V7X_SC_CORES * V7X_SC_SUBCORES
    ch = SC_CHUNK_ROWS
    per_worker = n // n_workers
    assert n % (n_workers * ch) == 0
    mesh = plsc.VectorSubcoreMesh(core_axis_name="c", subcore_axis_name="s")

    @functools.partial(
        pl.kernel, mesh=mesh,
        out_type=jax.ShapeDtypeStruct((n, d), table.dtype),
        scratch_types=[pltpu.VMEM((ch,), jnp.int32), pltpu.VMEM((ch, d), table.dtype),
                       pltpu.SemaphoreType.DMA],
        name="sc_gather_rows",
    )
    def gather(table_hbm, idx_hbm, out_hbm, idx_v, rows_v, sem):
        worker = lax.axis_index("s") * V7X_SC_CORES + lax.axis_index("c")
        base = worker * per_worker

        @pl.loop(0, per_worker // ch)
        def _(c):
            off = base + c * ch
            pltpu.sync_copy(idx_hbm.at[pl.ds(off, ch)], idx_v)
            pltpu.async_copy(table_hbm.at[idx_v], rows_v, sem).wait()
            pltpu.sync_copy(rows_v, out_hbm.at[pl.ds(off, ch)])

    return gather(table, idx)


def _sc_dispatch_rows(x, slot_flat, n_rows):
    t, d = x.shape
    n_workers = V7X_SC_CORES * V7X_SC_SUBCORES
    ch = SC_CHUNK_ROWS
    per_worker = t // n_workers
    assert t % (n_workers * ch) == 0 and slot_flat.shape == (TOP_K * t,)
    mesh = plsc.VectorSubcoreMesh(core_axis_name="c", subcore_axis_name="s")

    @functools.partial(
        pl.kernel, mesh=mesh,
        out_type=jax.ShapeDtypeStruct((n_rows, d), x.dtype),
        scratch_types=[pltpu.VMEM((ch,), jnp.int32)] * TOP_K + [
            pltpu.VMEM((ch, d), x.dtype), pltpu.SemaphoreType.DMA, pltpu.SemaphoreType.DMA],
        name="sc_dispatch_rows",
    )
    def dispatch(x_hbm, slot_hbm, out_hbm, *scratch):
        idx_vs, rows_v, idx_sem, row_sem = scratch[:TOP_K], scratch[TOP_K], scratch[TOP_K + 1], scratch[TOP_K + 2]
        worker = lax.axis_index("s") * V7X_SC_CORES + lax.axis_index("c")
        base = worker * per_worker

        @pl.loop(0, per_worker // ch)
        def _(c):
            off = base + c * ch
            idx_copies = [pltpu.async_copy(slot_hbm.at[pl.ds(k * t + off, ch)], idx_vs[k], idx_sem)
                          for k in range(TOP_K)]
            pltpu.sync_copy(x_hbm.at[pl.ds(off, ch)], rows_v)
            for cp in idx_copies:
                cp.wait()
            row_copies = [pltpu.async_copy(rows_v, out_hbm.at[idx_vs[k]], row_sem) for k in range(TOP_K)]
            for cp in row_copies:
                cp.wait()

    return dispatch(x, slot_flat)


def _experts_kernel(blk_e_ref, blk_valid_ref, first_ref, next_e_ref, n_used_ref,
                    xs_hbm, wg_hbm, wu_hbm, wd_hbm, ys_hbm,
                    xbuf, ybuf, wg_f, wu_f, wd_f, wg_b, wu_b, wd_b, in_sem, out_sem, w_sem, *, bm):
    n_used = n_used_ref[0]
    half = xbuf.shape[2]

    def x_copy(b, slot):
        return pltpu.make_async_copy(xs_hbm.at[pl.ds(b * bm, bm)], xbuf.at[slot], in_sem.at[slot])

    def y_copy(b, slot):
        return pltpu.make_async_copy(ybuf.at[slot], ys_hbm.at[pl.ds(b * bm, bm)], out_sem.at[slot])

    def w_copies(e, wslot):
        return [pltpu.make_async_copy(src.at[e], dst.at[wslot], w_sem.at[wslot])
                for src, dst in ((wg_hbm, wg_f), (wu_hbm, wu_f), (wd_hbm, wd_f))]

    n_in, n_out = xbuf.shape[0], ybuf.shape[0]
    for j in range(n_in - 1):
        @pl.when(j < n_used)
        def _():
            x_copy(j, j).start()
    for cp in w_copies(blk_e_ref[0], 0):
        cp.start()

    def block(b, w_next):
        slot = lax.rem(b, n_out)
        in_slot = lax.rem(b, n_in)
        ahead = b + (n_in - 1)

        @pl.when(ahead < n_used)
        def _():
            x_copy(ahead, lax.rem(ahead, n_in)).start()

        is_first = first_ref[b] == 1

        @pl.when(is_first)
        def _():
            for cp in w_copies(blk_e_ref[b], w_next):
                cp.wait()
            wg_b[...] = wg_f[w_next].astype(_BF16)
            wu_b[...] = wu_f[w_next].astype(_BF16)
            wd_b[...] = wd_f[w_next].astype(_BF16)

            @pl.when(next_e_ref[b] >= 0)
            def _():
                for cp in w_copies(next_e_ref[b], 1 - w_next):
                    cp.start()

        x_copy(b, in_slot).wait()

        @pl.when(b >= n_out)
        def _():
            y_copy(b - n_out, slot).wait()

        row = lax.broadcasted_iota(jnp.int32, (bm, half), 0)
        x_hi, x_lo = _unpack_bf16_pairs(jnp.where(row < blk_valid_ref[b], xbuf[in_slot], 0))
        x_hi, x_lo = x_hi.astype(_BF16), x_lo.astype(_BF16)
        g = _dot(x_hi, wg_b[:half, :]) + _dot(x_lo, wg_b[half:, :])
        u = _dot(x_hi, wu_b[:half, :]) + _dot(x_lo, wu_b[half:, :])
        h = (g * _sigmoid(g) * u).astype(_BF16)
        ybuf[slot] = _pack_bf16_pairs(_dot(h, wd_b[...]))
        y_copy(b, slot).start()
        return jnp.where(is_first, 1 - w_next, w_next)

    lax.fori_loop(0, n_used, block, jnp.int32(0))

    for j in range(n_out):
        @pl.when(n_used > j)
        def _():
            last = n_used - 1 - j
            y_copy(last, lax.rem(last, n_out)).wait()


def _experts_call(blk_e, blk_valid, blk_first, blk_next_e, n_used, xs, w_gate, w_up, w_down):
    n_rows, dp = xs.shape
    bm = EXPERT_ROWS
    d, f = w_gate.shape[-2:]
    assert d == 2 * dp
    any_spec = pl.BlockSpec(memory_space=pl.ANY)
    grid_spec = pltpu.PrefetchScalarGridSpec(
        num_scalar_prefetch=5,
        grid=(1,),
        in_specs=[any_spec] * 4,
        out_specs=any_spec,
        scratch_shapes=[
            pltpu.VMEM((EXPERT_IN_BUFFERS, bm, dp), jnp.int32),
            pltpu.VMEM((EXPERT_OUT_BUFFERS, bm, dp), jnp.int32),
            pltpu.VMEM((2, d, f), _F32), pltpu.VMEM((2, d, f), _F32), pltpu.VMEM((2, f, d), _F32),
            pltpu.VMEM((d, f), _BF16), pltpu.VMEM((d, f), _BF16), pltpu.VMEM((f, d), _BF16),
            pltpu.SemaphoreType.DMA((EXPERT_IN_BUFFERS,)), pltpu.SemaphoreType.DMA((EXPERT_OUT_BUFFERS,)),
            pltpu.SemaphoreType.DMA((2,)),
        ],
    )
    return pl.pallas_call(
        functools.partial(_experts_kernel, bm=bm),
        grid_spec=grid_spec,
        out_shape=jax.ShapeDtypeStruct((n_rows, dp), jnp.int32),
        compiler_params=pltpu.CompilerParams(
            dimension_semantics=("arbitrary",),
            vmem_limit_bytes=V7X_VMEM_LIMIT_BYTES),
        name="expert_ffn",
    )(blk_e, blk_valid, blk_first, blk_next_e, n_used, xs, w_gate, w_up, w_down)


def _combine_kernel(wts_ref, x1_ref, rows_ref, wsg_ref, wsu_ref, wsd_ref, g_ref, b_ref, out_ref):
    x1 = x1_ref[...]
    xb = x1.astype(_BF16)
    g = _dot(xb, wsg_ref[...])
    u = _dot(xb, wsu_ref[...])
    acc = ALPHA * x1 + _dot((g * _sigmoid(g) * u).astype(_BF16), wsd_ref[...])
    half = rows_ref.shape[2]
    acc_hi, acc_lo = acc[:, :half], acc[:, half:]
    w = wts_ref[...]
    for k in range(TOP_K):
        r_hi, r_lo = _unpack_bf16_pairs(rows_ref[k])
        acc_hi = acc_hi + r_hi * w[:, k:k + 1]
        acc_lo = acc_lo + r_lo * w[:, k:k + 1]
    acc = jnp.concatenate([acc_hi, acc_lo], axis=1)
    out_ref[...] = _layer_norm_rows(acc, g_ref[...], b_ref[...])


def _combine_call(wts_tk, x1, rows, ws_gate_bf, ws_up_bf, ws_down_bf, ln2_g, ln2_b):
    t, d = x1.shape
    tt = TOKEN_TILE
    fs = ws_gate_bf.shape[-1]

    def const(shape):
        return pl.BlockSpec(shape, lambda i: (0,) * len(shape))

    return pl.pallas_call(
        _combine_kernel,
        grid=(t // tt,),
        in_specs=[
            pl.BlockSpec((tt, TOP_K), lambda i: (i, 0)),
            pl.BlockSpec((tt, d), lambda i: (i, 0)),
            pl.BlockSpec((TOP_K, tt, d // 2), lambda i: (0, i, 0)),
            const((d, fs)), const((d, fs)), const((fs, d)),
            const((1, d)), const((1, d)),
        ],
        out_specs=pl.BlockSpec((tt, d), lambda i: (i, 0)),
        out_shape=jax.ShapeDtypeStruct((t, d), _F32),
        compiler_params=pltpu.CompilerParams(
            dimension_semantics=("arbitrary",),
            vmem_limit_bytes=V7X_VMEM_LIMIT_BYTES),
        name="combine_shared_norm",
    )(wts_tk, x1, rows, ws_gate_bf, ws_up_bf, ws_down_bf, ln2_g, ln2_b)


def _trunk(x, p):
    bsz, seq, d = x.shape
    t = bsz * seq
    x1, x1p, eid, wts, rank, cnt = _mixer_call(
        x.reshape(t, d), seq, p["w_in"], p["conv_a_w"], p["conv_a_b"], p["conv_b_w"], p["conv_b_b"],
        p["ln_c_g"], p["ln_c_b"], p["w_out_a"], p["w_out_b"], p["w_o"], p["ln1_g"], p["ln1_b"],
        p["wr_hi"], p["wr_lo"], p["rbias"], p["tri"])

    bm = EXPERT_ROWS
    counts = cnt[:, 0]
    pcounts = (counts + bm - 1) // bm * bm
    pend = jnp.cumsum(pcounts)
    pstart = pend - pcounts
    n_blocks = (t * TOP_K + N_EXPERTS * (bm - 1)) // bm
    n_used = jnp.maximum(pend[-1] // bm, 1).astype(jnp.int32)
    blk_start = jnp.minimum(jnp.arange(n_blocks, dtype=jnp.int32), n_used - 1) * bm
    blk_e = jnp.minimum(jnp.sum(pend[None, :] <= blk_start[:, None], axis=1), N_EXPERTS - 1).astype(jnp.int32)
    blk_valid = jnp.clip(counts[blk_e] - (blk_start - pstart[blk_e]), 0, bm).astype(jnp.int32)
    blk_first = (blk_start == pstart[blk_e]).astype(jnp.int32)
    owner = jnp.where(counts > 0, jnp.arange(N_EXPERTS, dtype=jnp.int32), N_EXPERTS)
    next_owner = lax.cummin(jnp.concatenate([owner[1:], jnp.full((1,), N_EXPERTS, jnp.int32)]), reverse=True)
    blk_next_e = jnp.where(next_owner < N_EXPERTS, next_owner, -1)[blk_e].astype(jnp.int32)
    pstart = pstart.astype(jnp.int32)

    slot = _slot_call(pstart, eid, rank).reshape(TOP_K * t)
    xs = _sc_dispatch_rows(x1p, slot, n_blocks * bm)
    ys = _experts_call(blk_e, blk_valid, blk_first, blk_next_e, n_used.reshape(1), xs,
                       p["w_gate"], p["w_up"], p["w_down"])
    rows = _sc_gather_rows(ys, slot).reshape(TOP_K, t, d // 2)
    out = _combine_call(wts.T, x1, rows, p["ws_gate"], p["ws_up"], p["ws_down"], p["ln2_g"], p["ln2_b"])
    return out.reshape(bsz, seq, d)


def _prepare_params(w_in, conv_a_w, conv_a_b, w_out_a, conv_b_w, conv_b_b, ln_c_g, ln_c_b,
                    w_out_b, w_o, ln1_g, ln1_b, w_router, router_bias, w_gate, w_up, w_down,
                    ws_gate, ws_up, ws_down, ln2_g, ln2_b):
    assert w_in.shape[0] == DEPTH == 1
    d = w_in.shape[1]
    wr = w_router[0].T
    wr_hi = wr.astype(_BF16)
    ts = SEQ_TILE
    return dict(
        w_in=w_in[0].astype(_BF16),
        conv_a_w=conv_a_w[0], conv_a_b=conv_a_b[0].reshape(1, d),
        conv_b_w=conv_b_w[0], conv_b_b=conv_b_b[0].reshape(1, d),
        ln_c_g=ln_c_g[0].reshape(1, d), ln_c_b=ln_c_b[0].reshape(1, d),
        w_out_a=w_out_a[0].astype(_BF16), w_out_b=w_out_b[0].astype(_BF16), w_o=w_o[0].astype(_BF16),
        ln1_g=ln1_g[0].reshape(1, d), ln1_b=ln1_b[0].reshape(1, d),
        wr_hi=wr_hi, wr_lo=(wr - wr_hi.astype(_F32)).astype(_BF16),
        rbias=router_bias[0].reshape(N_EXPERTS, 1),
        tri=(jnp.arange(ts)[:, None] < jnp.arange(ts)[None, :]).astype(_BF16),
        w_gate=w_gate[0], w_up=w_up[0], w_down=w_down[0],
        ws_gate=ws_gate[0].astype(_BF16), ws_up=ws_up[0].astype(_BF16), ws_down=ws_down[0].astype(_BF16),
        ln2_g=ln2_g[0].reshape(1, d), ln2_b=ln2_b[0].reshape(1, d),
    )


def kernel(x_prompt, x_sample, w_in, conv_a_w, conv_a_b, w_out_a, conv_b_w, conv_b_b, ln_c_g, ln_c_b,
           w_out_b, w_o, ln1_g, ln1_b, w_router, router_bias, w_gate, w_up, w_down, ws_gate, ws_up,
           ws_down, ln2_g, ln2_b):
    p = _prepare_params(w_in, conv_a_w, conv_a_b, w_out_a, conv_b_w, conv_b_b, ln_c_g, ln_c_b,
                        w_out_b, w_o, ln1_g, ln1_b, w_router, router_bias, w_gate, w_up, w_down,
                        ws_gate, ws_up, ws_down, ln2_g, ln2_b)
    return (_trunk(x_prompt, p), _trunk(x_sample, p))
```

```python
import functools

import jax
import jax.numpy as jnp
from jax import lax
from jax.experimental import pallas as pl
from jax.experimental.pallas import tpu as pltpu
from jax.experimental.pallas import tpu_sc as plsc

K_SHORT = 3
K_CONF = 31
N_EXPERTS = 256
TOP_K = 8
N_GROUPS = 8
TOPK_GROUPS = 4
GROUP_SIZE = N_EXPERTS // N_GROUPS
ROUTE_SCALE = 2.5
LN_EPS = 1e-5
DEPTH = 1
ALPHA = (2.0 * DEPTH) ** 0.25
N_IN_PARTS = 7

V7X_LANES = 128
V7X_SUBLANES = 8
V7X_VMEM_LIMIT_BYTES = 56 * 1024 * 1024

HALO = 16
SEQ_TILE = 512
COL_CHUNK = 256
EXPERT_ROWS = 256
EXPERT_IN_BUFFERS = 6
EXPERT_OUT_BUFFERS = 3
TOKEN_TILE = 512
SLOT_TILE = 2048

V7X_SC_CORES = 2
V7X_SC_SUBCORES = 16
SC_CHUNK_ROWS = 64

_F32 = jnp.float32
_BF16 = jnp.bfloat16
_HIGH_HALF = -65536


def _dot(a, b):
    return jnp.dot(a, b, preferred_element_type=_F32)


def _dot_nt(a, b):
    return lax.dot_general(a, b, (((1,), (1,)), ((), ())), preferred_element_type=_F32)


def _sigmoid(v):
    return 1.0 / (1.0 + jnp.exp(-v))


def _pack_bf16_pairs(v):
    h = v.shape[1] // 2
    hi = lax.bitcast_convert_type(v[:, :h].astype(_BF16).astype(_F32), jnp.int32)
    lo = lax.bitcast_convert_type(v[:, h:].astype(_BF16).astype(_F32), jnp.int32)
    return (hi & _HIGH_HALF) | lax.shift_right_logical(lo, 16)


def _unpack_bf16_pairs(p):
    hi = lax.bitcast_convert_type(p & _HIGH_HALF, _F32)
    lo = lax.bitcast_convert_type(lax.shift_left(p, 16), _F32)
    return hi, lo


def _layer_norm_rows(v, g, b):
    mu = jnp.mean(v, axis=-1, keepdims=True)
    d = v - mu
    var = jnp.mean(d * d, axis=-1, keepdims=True)
    return d * lax.rsqrt(var + LN_EPS) * g + b


def _route(logits_t, rbias, ts):
    scores = _sigmoid(logits_t)
    biased = scores + rbias
    neg = -jnp.inf
    iota_g = lax.broadcasted_iota(jnp.int32, (GROUP_SIZE, ts), 0)
    gscore = []
    for g in range(N_GROUPS):
        v = biased[g * GROUP_SIZE:(g + 1) * GROUP_SIZE]
        m1 = jnp.max(v, axis=0, keepdims=True)
        i1 = jnp.min(jnp.where(v == m1, iota_g, GROUP_SIZE), axis=0, keepdims=True)
        m2 = jnp.max(jnp.where(iota_g == i1, neg, v), axis=0, keepdims=True)
        gscore.append(m1 + m2)
    masked = []
    for g in range(N_GROUPS):
        ahead = jnp.zeros((1, ts), jnp.int32)
        for h in range(N_GROUPS):
            if h == g:
                continue
            before = gscore[h] > gscore[g]
            if h < g:
                before = before | (gscore[h] == gscore[g])
            ahead = ahead + before.astype(jnp.int32)
        keep = ahead < TOPK_GROUPS
        masked.append(jnp.where(keep, biased[g * GROUP_SIZE:(g + 1) * GROUP_SIZE], neg))
    v = jnp.concatenate(masked, axis=0)

    iota_e = lax.broadcasted_iota(jnp.int32, (N_EXPERTS, ts), 0)
    ids, wts = [], []
    onehot = jnp.zeros((N_EXPERTS, ts), _F32)
    for _ in range(TOP_K):
        m = jnp.max(v, axis=0, keepdims=True)
        idx = jnp.min(jnp.where(v == m, iota_e, N_EXPERTS), axis=0, keepdims=True)
        hit = iota_e == idx
        wts.append(jnp.sum(jnp.where(hit, scores, 0.0), axis=0, keepdims=True))
        ids.append(idx)
        v = jnp.where(hit, neg, v)
        onehot = jnp.where(hit, 1.0, onehot)
    total = wts[0]
    for w in wts[1:]:
        total = total + w
    wts = [w / total * ROUTE_SCALE for w in wts]
    return ids, wts, onehot


def _mixer_kernel(xm_ref, xp_ref, xn_ref,
                  wh_ref, wb_ref, wc_ref, wv_ref, wg_ref, wga_ref, wgb_ref,
                  caw_ref, cab_ref, cbw_ref, cbb_ref, lncg_ref, lncb_ref,
                  woa_ref, wob_ref, wo_ref, ln1g_ref, ln1b_ref,
                  wrh_ref, wrl_ref, rbias_ref, tri_ref,
                  x1_ref, x1p_ref, eid_ref, wts_ref, rank_ref, cnt_ref,
                  xe_s, pa_s, z_s, ga_s, gb_s, ch_s, zsh_s, base_s,
                  *, ts, cw, nc, tiles_per_seq):
    i = pl.program_id(0)
    c = pl.program_id(1)
    pos = lax.rem(i, tiles_per_seq)

    @pl.when((i == 0) & (c == 0))
    def _():
        base_s[...] = jnp.zeros_like(base_s)

    @pl.when(c == 0)
    def _():
        keep_prev = (pos > 0).astype(_F32)
        keep_next = (pos < tiles_per_seq - 1).astype(_F32)
        xe_s[0:HALO, :] = (xp_ref[...] * keep_prev).astype(_BF16)
        xe_s[HALO:HALO + ts, :] = xm_ref[...].astype(_BF16)
        xe_s[HALO + ts:, :] = (xn_ref[...] * keep_next).astype(_BF16)

    xe = xe_s[...]
    xc = xe_s[HALO:HALO + ts, :]

    zsh_s[0] = _dot(xe, wv_ref[...]) * _sigmoid(_dot(xe, wg_ref[...]))
    span = zsh_s.shape[1] - V7X_SUBLANES
    for r in range(1, V7X_SUBLANES):
        zsh_s[r, 0:span, :] = zsh_s[0, pl.ds(r, span), :]

    ch_s[...] = _dot(xe, wh_ref[...]) * _dot(xe, wc_ref[...])
    u_b = _dot(xc, wb_ref[...])
    ga_s[c] = _sigmoid(_dot(xc, wga_ref[...]))
    gb_s[c] = _sigmoid(_dot(xc, wgb_ref[...]))

    conv_b = cbb_ref[...]
    for j in range(K_CONF):
        first = HALO - K_CONF // 2 + j
        aligned = first // V7X_SUBLANES * V7X_SUBLANES
        conv_b = conv_b + cbw_ref[j:j + 1, :] * zsh_s[first - aligned, pl.ds(aligned, ts), :]
    z_s[c] = conv_b

    conv_a = cab_ref[...]
    for j in range(K_SHORT):
        conv_a = conv_a + caw_ref[j:j + 1, :] * ch_s[pl.ds(HALO - K_SHORT // 2 + j, ts), :]
    pa_s[c] = (u_b * conv_a).astype(_BF16)

    @pl.when(c == nc - 1)
    def _():
        d_model = nc * cw
        mu = sum(jnp.sum(z_s[k], axis=-1, keepdims=True) for k in range(nc)) / d_model
        var = sum(jnp.sum((z_s[k] - mu) ** 2, axis=-1, keepdims=True) for k in range(nc)) / d_model
        inv = lax.rsqrt(var + LN_EPS)
        y_a = jnp.zeros((ts, d_model), _F32)
        y_b = jnp.zeros((ts, d_model), _F32)
        for k in range(nc):
            cols = slice(k * cw, (k + 1) * cw)
            zn = (z_s[k] - mu) * inv * lncg_ref[:, cols] + lncb_ref[:, cols]
            sw = zn * _sigmoid(zn)
            y_b = y_b + _dot(sw.astype(_BF16), wob_ref[cols, :])
            y_a = y_a + _dot(pa_s[k], woa_ref[cols, :])
        mix = jnp.zeros((ts, d_model), _F32)
        for k in range(nc):
            cols = slice(k * cw, (k + 1) * cw)
            merged = ga_s[k] * y_a[:, cols] + gb_s[k] * y_b[:, cols]
            mix = mix + _dot(merged.astype(_BF16), wo_ref[cols, :])
        x1 = _layer_norm_rows(ALPHA * xm_ref[...] + mix, ln1g_ref[...], ln1b_ref[...])
        x1_ref[...] = x1
        x1p_ref[...] = _pack_bf16_pairs(x1)

        xh = x1.astype(_BF16)
        xl = (x1 - xh.astype(_F32)).astype(_BF16)
        logits_t = _dot_nt(wrh_ref[...], xh) + (_dot_nt(wrl_ref[...], xh) + _dot_nt(wrh_ref[...], xl))
        ids, wts, onehot = _route(logits_t, rbias_ref[...], ts)

        before = _dot(onehot.astype(_BF16), tri_ref[...]) + base_s[:, 0:1]
        iota_e = lax.broadcasted_iota(jnp.int32, (N_EXPERTS, ts), 0)
        for k in range(TOP_K):
            eid_ref[k:k + 1, :] = ids[k]
            wts_ref[k:k + 1, :] = wts[k]
            r = jnp.sum(jnp.where(iota_e == ids[k], before, 0.0), axis=0, keepdims=True)
            rank_ref[k:k + 1, :] = r.astype(jnp.int32)
        base_s[...] = base_s[...] + jnp.sum(onehot, axis=1, keepdims=True)
        cnt_ref[...] = base_s[...].astype(jnp.int32)


def _mixer_call(x2d, seq_len, w_in_bf, conv_a_w, conv_a_b, conv_b_w, conv_b_b, ln_c_g, ln_c_b,
                w_out_a_bf, w_out_b_bf, w_o_bf, ln1_g, ln1_b, wr_hi, wr_lo, rbias, tri):
    t, d = x2d.shape
    ts, cw = SEQ_TILE, COL_CHUNK
    nc = d // cw
    assert seq_len % ts == 0 and d % cw == 0 and ts % HALO == 0
    n_tiles = t // ts
    halo_blocks_per_tile = ts // HALO
    n_halo_blocks = t // HALO

    def part_spec(p):
        return pl.BlockSpec((d, cw), lambda i, c, p=p: (0, p * nc + c))

    def chunk_rows(rows):
        return pl.BlockSpec((rows, cw), lambda i, c: (0, c))

    def const(shape):
        return pl.BlockSpec(shape, lambda i, c: (0,) * len(shape))

    tok_rows = lambda rows: pl.BlockSpec((rows, ts), lambda i, c: (0, i))

    in_specs = [
        pl.BlockSpec((ts, d), lambda i, c: (i, 0)),
        pl.BlockSpec((HALO, d), lambda i, c: (jnp.maximum(i * halo_blocks_per_tile - 1, 0), 0)),
        pl.BlockSpec((HALO, d),
                     lambda i, c: (jnp.minimum((i + 1) * halo_blocks_per_tile, n_halo_blocks - 1), 0)),
    ] + [part_spec(p) for p in range(N_IN_PARTS)] + [
        chunk_rows(K_SHORT), chunk_rows(1), chunk_rows(K_CONF), chunk_rows(1),
        const((1, d)), const((1, d)),
        const((d, d)), const((d, d)), const((d, d)),
        const((1, d)), const((1, d)),
        const((N_EXPERTS, d)), const((N_EXPERTS, d)), const((N_EXPERTS, 1)),
        const((ts, ts)),
    ]
    out_shape = [
        jax.ShapeDtypeStruct((t, d), _F32),
        jax.ShapeDtypeStruct((t, d // 2), jnp.int32),
        jax.ShapeDtypeStruct((TOP_K, t), jnp.int32),
        jax.ShapeDtypeStruct((TOP_K, t), _F32),
        jax.ShapeDtypeStruct((TOP_K, t), jnp.int32),
        jax.ShapeDtypeStruct((N_EXPERTS, V7X_LANES), jnp.int32),
    ]
    out_specs = [
        pl.BlockSpec((ts, d), lambda i, c: (i, 0)),
        pl.BlockSpec((ts, d // 2), lambda i, c: (i, 0)),
        tok_rows(TOP_K), tok_rows(TOP_K), tok_rows(TOP_K),
        const((N_EXPERTS, V7X_LANES)),
    ]
    scratch = [
        pltpu.VMEM((ts + 2 * HALO, d), _BF16),
        pltpu.VMEM((nc, ts, cw), _BF16),
        pltpu.VMEM((nc, ts, cw), _F32),
        pltpu.VMEM((nc, ts, cw), _F32),
        pltpu.VMEM((nc, ts, cw), _F32),
        pltpu.VMEM((ts + 2 * HALO, cw), _F32),
        pltpu.VMEM((V7X_SUBLANES, ts + 2 * HALO, cw), _F32),
        pltpu.VMEM((N_EXPERTS, V7X_LANES), _F32),
    ]
    kern = functools.partial(_mixer_kernel, ts=ts, cw=cw, nc=nc, tiles_per_seq=seq_len // ts)
    w_parts = [w_in_bf] * N_IN_PARTS
    return pl.pallas_call(
        kern,
        grid=(n_tiles, nc),
        in_specs=in_specs,
        out_specs=out_specs,
        out_shape=out_shape,
        scratch_shapes=scratch,
        compiler_params=pltpu.CompilerParams(
            dimension_semantics=("arbitrary", "arbitrary"),
            vmem_limit_bytes=V7X_VMEM_LIMIT_BYTES),
        name="mixer_router",
    )(x2d, x2d, x2d, *w_parts, conv_a_w, conv_a_b, conv_b_w, conv_b_b, ln_c_g, ln_c_b,
      w_out_a_bf, w_out_b_bf, w_o_bf, ln1_g, ln1_b, wr_hi, wr_lo, rbias, tri)


def _slot_kernel(pstart_ref, eid_ref, rank_ref, slot_ref):
    eid = eid_ref[...]

    def body(e, acc):
        return jnp.where(eid == e, pstart_ref[e], acc)

    slot_ref[...] = lax.fori_loop(0, N_EXPERTS, body, jnp.zeros_like(eid), unroll=8) + rank_ref[...]


def _slot_call(pstart, eid, rank):
    t = eid.shape[1]
    tl = min(SLOT_TILE, t)
    spec = pl.BlockSpec((TOP_K, tl), lambda i, *_: (0, i))
    return pl.pallas_call(
        _slot_kernel,
        grid_spec=pltpu.PrefetchScalarGridSpec(
            num_scalar_prefetch=1, grid=(t // tl,), in_specs=[spec, spec], out_specs=spec),
        out_shape=jax.ShapeDtypeStruct(eid.shape, jnp.int32),
        compiler_params=pltpu.CompilerParams(dimension_semantics=("arbitrary",)),
        name="slot_lookup",
    )(pstart, eid, rank)


def _sc_gather_rows(table, idx):
    n, = idx.shape
    d = table.shape[1]
    n_workers = V7X_SC_CORES * V7X_SC_SUBCORES
    ch = SC_CHUNK_ROWS
    per_worker = n // n_workers
    n_chunks = per_worker // ch
    assert n % (n_workers * ch) == 0 and n_chunks % 2 == 0
    mesh = plsc.VectorSubcoreMesh(core_axis_name="c", subcore_axis_name="s")

    @functools.partial(
        pl.kernel, mesh=mesh,
        out_type=jax.ShapeDtypeStruct((n, d), table.dtype),
        scratch_types=[pltpu.VMEM((ch,), jnp.int32), pltpu.VMEM((ch,), jnp.int32),
                       pltpu.VMEM((ch, d), table.dtype), pltpu.VMEM((ch, d), table.dtype),
                       pltpu.SemaphoreType.DMA, pltpu.SemaphoreType.DMA],
        name="sc_gather_rows",
    )
    def gather(table_hbm, idx_hbm, out_hbm, idx_v0, idx_v1, rows_v0, rows_v1, sem0, sem1):
        worker = lax.axis_index("s") * V7X_SC_CORES + lax.axis_index("c")
        base = worker * per_worker
        bufs = ((idx_v0, rows_v0, sem0), (idx_v1, rows_v1, sem1))

        def fetch(c, buf):
            idx_v, rows_v, sem = buf
            pltpu.sync_copy(idx_hbm.at[pl.ds(base + c * ch, ch)], idx_v)
            pltpu.make_async_copy(table_hbm.at[idx_v], rows_v, sem).start()

        def drain(c, buf):
            idx_v, rows_v, sem = buf
            pltpu.make_async_copy(table_hbm.at[idx_v], rows_v, sem).wait()
            pltpu.sync_copy(rows_v, out_hbm.at[pl.ds(base + c * ch, ch)])

        fetch(0, bufs[0])

        @pl.loop(0, n_chunks, step=2)
        def _(c):
            fetch(c + 1, bufs[1])
            drain(c, bufs[0])

            @pl.when(c + 2 < n_chunks)
            def _():
                fetch(c + 2, bufs[0])

            drain(c + 1, bufs[1])

    return gather(table, idx)


def _sc_dispatch_rows(x, slot_flat, n_rows):
    t, d = x.shape
    n_workers = V7X_SC_CORES * V7X_SC_SUBCORES
    ch = SC_CHUNK_ROWS
    per_worker = t // n_workers
    assert t % (n_workers * ch) == 0 and slot_flat.shape == (TOP_K * t,)
    mesh = plsc.VectorSubcoreMesh(core_axis_name="c", subcore_axis_name="s")

    @functools.partial(
        pl.kernel, mesh=mesh,
        out_type=jax.ShapeDtypeStruct((n_rows, d), x.dtype),
        scratch_types=[pltpu.VMEM((ch,), jnp.int32)] * TOP_K + [
            pltpu.VMEM((ch, d), x.dtype), pltpu.SemaphoreType.DMA, pltpu.SemaphoreType.DMA],
        name="sc_dispatch_rows",
    )
    def dispatch(x_hbm, slot_hbm, out_hbm, *scratch):
        idx_vs, rows_v, idx_sem, row_sem = scratch[:TOP_K], scratch[TOP_K], scratch[TOP_K + 1], scratch[TOP_K + 2]
        worker = lax.axis_index("s") * V7X_SC_CORES + lax.axis_index("c")
        base = worker * per_worker

        @pl.loop(0, per_worker // ch)
        def _(c):
            off = base + c * ch
            idx_copies = [pltpu.async_copy(slot_hbm.at[pl.ds(k * t + off, ch)], idx_vs[k], idx_sem)
                          for k in range(TOP_K)]
            pltpu.sync_copy(x_hbm.at[pl.ds(off, ch)], rows_v)
            for cp in idx_copies:
                cp.wait()
            row_copies = [pltpu.async_copy(rows_v, out_hbm.at[idx_vs[k]], row_sem) for k in range(TOP_K)]
            for cp in row_copies:
                cp.wait()

    return dispatch(x, slot_flat)


def _experts_kernel(blk_e_ref, blk_valid_ref, first_ref, next_e_ref, n_used_ref,
                    xs_hbm, wg_hbm, wu_hbm, wd_hbm, ys_hbm,
                    xbuf, ybuf, wg_f, wu_f, wd_f, wg_b, wu_b, wd_b, in_sem, out_sem, w_sem, *, bm):
    n_used = n_used_ref[0]
    half = xbuf.shape[2]

    def x_copy(b, slot):
        return pltpu.make_async_copy(xs_hbm.at[pl.ds(b * bm, bm)], xbuf.at[slot], in_sem.at[slot])

    def y_copy(b, slot):
        return pltpu.make_async_copy(ybuf.at[slot], ys_hbm.at[pl.ds(b * bm, bm)], out_sem.at[slot])

    def w_copies(e, wslot):
        return [pltpu.make_async_copy(src.at[e], dst.at[wslot], w_sem.at[wslot])
                for src, dst in ((wg_hbm, wg_f), (wu_hbm, wu_f), (wd_hbm, wd_f))]

    n_in, n_out = xbuf.shape[0], ybuf.shape[0]
    for j in range(n_in - 1):
        @pl.when(j < n_used)
        def _():
            x_copy(j, j).start()
    for cp in w_copies(blk_e_ref[0], 0):
        cp.start()

    def block(b, w_next):
        slot = lax.rem(b, n_out)
        in_slot = lax.rem(b, n_in)
        ahead = b + (n_in - 1)

        @pl.when(ahead < n_used)
        def _():
            x_copy(ahead, lax.rem(ahead, n_in)).start()

        is_first = first_ref[b] == 1

        @pl.when(is_first)
        def _():
            for cp in w_copies(blk_e_ref[b], w_next):
                cp.wait()
            wg_b[...] = wg_f[w_next].astype(_BF16)
            wu_b[...] = wu_f[w_next].astype(_BF16)
            wd_b[...] = wd_f[w_next].astype(_BF16)

            @pl.when(next_e_ref[b] >= 0)
            def _():
                for cp in w_copies(next_e_ref[b], 1 - w_next):
                    cp.start()

        x_copy(b, in_slot).wait()

        @pl.when(b >= n_out)
        def _():
            y_copy(b - n_out, slot).wait()

        row = lax.broadcasted_iota(jnp.int32, (bm, half), 0)
        x_hi, x_lo = _unpack_bf16_pairs(jnp.where(row < blk_valid_ref[b], xbuf[in_slot], 0))
        x_hi, x_lo = x_hi.astype(_BF16), x_lo.astype(_BF16)
        g = _dot(x_hi, wg_b[:half, :]) + _dot(x_lo, wg_b[half:, :])
        u = _dot(x_hi, wu_b[:half, :]) + _dot(x_lo, wu_b[half:, :])
        h = (g * _sigmoid(g) * u).astype(_BF16)
        ybuf[slot] = _pack_bf16_pairs(_dot(h, wd_b[...]))
        y_copy(b, slot).start()
        return jnp.where(is_first, 1 - w_next, w_next)

    lax.fori_loop(0, n_used, block, jnp.int32(0))

    for j in range(n_out):
        @pl.when(n_used > j)
        def _():
            last = n_used - 1 - j
            y_copy(last, lax.rem(last, n_out)).wait()


def _experts_call(blk_e, blk_valid, blk_first, blk_next_e, n_used, xs, w_gate, w_up, w_down):
    n_rows, dp = xs.shape
    bm = EXPERT_ROWS
    d, f = w_gate.shape[-2:]
    assert d == 2 * dp
    any_spec = pl.BlockSpec(memory_space=pl.ANY)
    grid_spec = pltpu.PrefetchScalarGridSpec(
        num_scalar_prefetch=5,
        grid=(1,),
        in_specs=[any_spec] * 4,
        out_specs=any_spec,
        scratch_shapes=[
            pltpu.VMEM((EXPERT_IN_BUFFERS, bm, dp), jnp.int32),
            pltpu.VMEM((EXPERT_OUT_BUFFERS, bm, dp), jnp.int32),
            pltpu.VMEM((2, d, f), _F32), pltpu.VMEM((2, d, f), _F32), pltpu.VMEM((2, f, d), _F32),
            pltpu.VMEM((d, f), _BF16), pltpu.VMEM((d, f), _BF16), pltpu.VMEM((f, d), _BF16),
            pltpu.SemaphoreType.DMA((EXPERT_IN_BUFFERS,)), pltpu.SemaphoreType.DMA((EXPERT_OUT_BUFFERS,)),
            pltpu.SemaphoreType.DMA((2,)),
        ],
    )
    return pl.pallas_call(
        functools.partial(_experts_kernel, bm=bm),
        grid_spec=grid_spec,
        out_shape=jax.ShapeDtypeStruct((n_rows, dp), jnp.int32),
        compiler_params=pltpu.CompilerParams(
            dimension_semantics=("arbitrary",),
            vmem_limit_bytes=V7X_VMEM_LIMIT_BYTES),
        name="expert_ffn",
    )(blk_e, blk_valid, blk_first, blk_next_e, n_used, xs, w_gate, w_up, w_down)


def _combine_kernel(wts_ref, x1_ref, rows_ref, wsg_ref, wsu_ref, wsd_ref, g_ref, b_ref, out_ref):
    x1 = x1_ref[...]
    xb = x1.astype(_BF16)
    g = _dot(xb, wsg_ref[...])
    u = _dot(xb, wsu_ref[...])
    acc = ALPHA * x1 + _dot((g * _sigmoid(g) * u).astype(_BF16), wsd_ref[...])
    half = rows_ref.shape[2]
    acc_hi, acc_lo = acc[:, :half], acc[:, half:]
    w = wts_ref[...]
    for k in range(TOP_K):
        r_hi, r_lo = _unpack_bf16_pairs(rows_ref[k])
        acc_hi = acc_hi + r_hi * w[:, k:k + 1]
        acc_lo = acc_lo + r_lo * w[:, k:k + 1]
    acc = jnp.concatenate([acc_hi, acc_lo], axis=1)
    out_ref[...] = _layer_norm_rows(acc, g_ref[...], b_ref[...])


def _combine_call(wts_tk, x1, rows, ws_gate_bf, ws_up_bf, ws_down_bf, ln2_g, ln2_b):
    t, d = x1.shape
    tt = TOKEN_TILE
    fs = ws_gate_bf.shape[-1]

    def const(shape):
        return pl.BlockSpec(shape, lambda i: (0,) * len(shape))

    return pl.pallas_call(
        _combine_kernel,
        grid=(t // tt,),
        in_specs=[
            pl.BlockSpec((tt, TOP_K), lambda i: (i, 0)),
            pl.BlockSpec((tt, d), lambda i: (i, 0)),
            pl.BlockSpec((TOP_K, tt, d // 2), lambda i: (0, i, 0)),
            const((d, fs)), const((d, fs)), const((fs, d)),
            const((1, d)), const((1, d)),
        ],
        out_specs=pl.BlockSpec((tt, d), lambda i: (i, 0)),
        out_shape=jax.ShapeDtypeStruct((t, d), _F32),
        compiler_params=pltpu.CompilerParams(
            dimension_semantics=("arbitrary",),
            vmem_limit_bytes=V7X_VMEM_LIMIT_BYTES),
        name="combine_shared_norm",
    )(wts_tk, x1, rows, ws_gate_bf, ws_up_bf, ws_down_bf, ln2_g, ln2_b)


def _trunk(x, p):
    bsz, seq, d = x.shape
    t = bsz * seq
    x1, x1p, eid, wts, rank, cnt = _mixer_call(
        x.reshape(t, d), seq, p["w_in"], p["conv_a_w"], p["conv_a_b"], p["conv_b_w"], p["conv_b_b"],
        p["ln_c_g"], p["ln_c_b"], p["w_out_a"], p["w_out_b"], p["w_o"], p["ln1_g"], p["ln1_b"],
        p["wr_hi"], p["wr_lo"], p["rbias"], p["tri"])

    bm = EXPERT_ROWS
    counts = cnt[:, 0]
    pcounts = (counts + bm - 1) // bm * bm
    pend = jnp.cumsum(pcounts)
    pstart = pend - pcounts
    n_blocks = (t * TOP_K + N_EXPERTS * (bm - 1)) // bm
    n_used = jnp.maximum(pend[-1] // bm, 1).astype(jnp.int32)
    blk_start = jnp.minimum(jnp.arange(n_blocks, dtype=jnp.int32), n_used - 1) * bm
    blk_e = jnp.minimum(jnp.sum(pend[None, :] <= blk_start[:, None], axis=1), N_EXPERTS - 1).astype(jnp.int32)
    blk_valid = jnp.clip(counts[blk_e] - (blk_start - pstart[blk_e]), 0, bm).astype(jnp.int32)
    blk_first = (blk_start == pstart[blk_e]).astype(jnp.int32)
    owner = jnp.where(counts > 0, jnp.arange(N_EXPERTS, dtype=jnp.int32), N_EXPERTS)
    next_owner = lax.cummin(jnp.concatenate([owner[1:], jnp.full((1,), N_EXPERTS, jnp.int32)]), reverse=True)
    blk_next_e = jnp.where(next_owner < N_EXPERTS, next_owner, -1)[blk_e].astype(jnp.int32)
    pstart = pstart.astype(jnp.int32)

    slot = _slot_call(pstart, eid, rank).reshape(TOP_K * t)
    xs = _sc_dispatch_rows(x1p, slot, n_blocks * bm)
    ys = _experts_call(blk_e, blk_valid, blk_first, blk_next_e, n_used.reshape(1), xs,
                       p["w_gate"], p["w_up"], p["w_down"])
    rows = _sc_gather_rows(ys, slot).reshape(TOP_K, t, d // 2)
    out = _combine_call(wts.T, x1, rows, p["ws_gate"], p["ws_up"], p["ws_down"], p["ln2_g"], p["ln2_b"])
    return out.reshape(bsz, seq, d)


def _prepare_params(w_in, conv_a_w, conv_a_b, w_out_a, conv_b_w, conv_b_b, ln_c_g, ln_c_b,
                    w_out_b, w_o, ln1_g, ln1_b, w_router, router_bias, w_gate, w_up, w_down,
                    ws_gate, ws_up, ws_down, ln2_g, ln2_b):
    assert w_in.shape[0] == DEPTH == 1
    d = w_in.shape[1]
    wr = w_router[0].T
    wr_hi = wr.astype(_BF16)
    ts = SEQ_TILE
    return dict(
        w_in=w_in[0].astype(_BF16),
        conv_a_w=conv_a_w[0], conv_a_b=conv_a_b[0].reshape(1, d),
        conv_b_w=conv_b_w[0], conv_b_b=conv_b_b[0].reshape(1, d),
        ln_c_g=ln_c_g[0].reshape(1, d), ln_c_b=ln_c_b[0].reshape(1, d),
        w_out_a=w_out_a[0].astype(_BF16), w_out_b=w_out_b[0].astype(_BF16), w_o=w_o[0].astype(_BF16),
        ln1_g=ln1_g[0].reshape(1, d), ln1_b=ln1_b[0].reshape(1, d),
        wr_hi=wr_hi, wr_lo=(wr - wr_hi.astype(_F32)).astype(_BF16),
        rbias=router_bias[0].reshape(N_EXPERTS, 1),
        tri=(jnp.arange(ts)[:, None] < jnp.arange(ts)[None, :]).astype(_BF16),
        w_gate=w_gate[0], w_up=w_up[0], w_down=w_down[0],
        ws_gate=ws_gate[0].astype(_BF16), ws_up=ws_up[0].astype(_BF16), ws_down=ws_down[0].astype(_BF16),
        ln2_g=ln2_g[0].reshape(1, d), ln2_b=ln2_b[0].reshape(1, d),
    )


def kernel(x_prompt, x_sample, w_in, conv_a_w, conv_a_b, w_out_a, conv_b_w, conv_b_b, ln_c_g, ln_c_b,
           w_out_b, w_o, ln1_g, ln1_b, w_router, router_bias, w_gate, w_up, w_down, ws_gate, ws_up,
           ws_down, ln2_g, ln2_b):
    p = _prepare_params(w_in, conv_a_w, conv_a_b, w_out_a, conv_b_w, conv_b_b, ln_c_g, ln_c_b,
                        w_out_b, w_o, ln1_g, ln1_b, w_router, router_bias, w_gate, w_up, w_down,
                        ws_gate, ws_up, ws_down, ln2_g, ln2_b)
    return (_trunk(x_prompt, p), _trunk(x_sample, p))
```

```python
import functools

import jax
import jax.numpy as jnp
from jax import lax
from jax.experimental import pallas as pl
from jax.experimental.pallas import tpu as pltpu
from jax.experimental.pallas import tpu_sc as plsc

K_SHORT = 3
K_CONF = 31
N_EXPERTS = 256
TOP_K = 8
N_GROUPS = 8
TOPK_GROUPS = 4
GROUP_SIZE = N_EXPERTS // N_GROUPS
ROUTE_SCALE = 2.5
LN_EPS = 1e-5
DEPTH = 1
ALPHA = (2.0 * DEPTH) ** 0.25
N_IN_PARTS = 7

V7X_LANES = 128
V7X_SUBLANES = 8
V7X_VMEM_LIMIT_BYTES = 56 * 1024 * 1024

HALO = 16
SEQ_TILE = 512
COL_CHUNK = 256
EXPERT_ROWS = 256
EXPERT_IN_BUFFERS = 6
EXPERT_OUT_BUFFERS = 3
TOKEN_TILE = 512
SLOT_TILE = 2048

V7X_SC_CORES = 2
V7X_SC_SUBCORES = 16
SC_CHUNK_ROWS = 64

_F32 = jnp.float32
_BF16 = jnp.bfloat16
_HIGH_HALF = -65536


def _dot(a, b):
    return jnp.dot(a, b, preferred_element_type=_F32)


def _dot_nt(a, b):
    return lax.dot_general(a, b, (((1,), (1,)), ((), ())), preferred_element_type=_F32)


def _sigmoid(v):
    return 1.0 / (1.0 + jnp.exp(-v))


def _pack_bf16_pairs(v):
    h = v.shape[1] // 2
    hi = lax.bitcast_convert_type(v[:, :h].astype(_BF16).astype(_F32), jnp.int32)
    lo = lax.bitcast_convert_type(v[:, h:].astype(_BF16).astype(_F32), jnp.int32)
    return (hi & _HIGH_HALF) | lax.shift_right_logical(lo, 16)


def _unpack_bf16_pairs(p):
    hi = lax.bitcast_convert_type(p & _HIGH_HALF, _F32)
    lo = lax.bitcast_convert_type(lax.shift_left(p, 16), _F32)
    return hi, lo


def _layer_norm_rows(v, g, b):
    mu = jnp.mean(v, axis=-1, keepdims=True)
    d = v - mu
    var = jnp.mean(d * d, axis=-1, keepdims=True)
    return d * lax.rsqrt(var + LN_EPS) * g + b


def _route(logits_t, rbias, ts):
    scores = _sigmoid(logits_t)
    biased = scores + rbias
    neg = -jnp.inf
    iota_g = lax.broadcasted_iota(jnp.int32, (GROUP_SIZE, ts), 0)
    gscore = []
    for g in range(N_GROUPS):
        v = biased[g * GROUP_SIZE:(g + 1) * GROUP_SIZE]
        m1 = jnp.max(v, axis=0, keepdims=True)
        i1 = jnp.min(jnp.where(v == m1, iota_g, GROUP_SIZE), axis=0, keepdims=True)
        m2 = jnp.max(jnp.where(iota_g == i1, neg, v), axis=0, keepdims=True)
        gscore.append(m1 + m2)
    masked = []
    for g in range(N_GROUPS):
        ahead = jnp.zeros((1, ts), jnp.int32)
        for h in range(N_GROUPS):
            if h == g:
                continue
            before = gscore[h] > gscore[g]
            if h < g:
                before = before | (gscore[h] == gscore[g])
            ahead = ahead + before.astype(jnp.int32)
        keep = ahead < TOPK_GROUPS
        masked.append(jnp.where(keep, biased[g * GROUP_SIZE:(g + 1) * GROUP_SIZE], neg))
    v = jnp.concatenate(masked, axis=0)

    iota_e = lax.broadcasted_iota(jnp.int32, (N_EXPERTS, ts), 0)
    ids, wts = [], []
    onehot = jnp.zeros((N_EXPERTS, ts), _F32)
    for _ in range(TOP_K):
        m = jnp.max(v, axis=0, keepdims=True)
        idx = jnp.min(jnp.where(v == m, iota_e, N_EXPERTS), axis=0, keepdims=True)
        hit = iota_e == idx
        wts.append(jnp.sum(jnp.where(hit, scores, 0.0), axis=0, keepdims=True))
        ids.append(idx)
        v = jnp.where(hit, neg, v)
        onehot = jnp.where(hit, 1.0, onehot)
    total = wts[0]
    for w in wts[1:]:
        total = total + w
    wts = [w / total * ROUTE_SCALE for w in wts]
    return ids, wts, onehot


def _mixer_kernel(xm_ref, xp_ref, xn_ref,
                  wh_ref, wb_ref, wc_ref, wv_ref, wg_ref, wga_ref, wgb_ref,
                  caw_ref, cab_ref, cbw_ref, cbb_ref, lncg_ref, lncb_ref,
                  woa_ref, wob_ref, wo_ref, ln1g_ref, ln1b_ref,
                  wrh_ref, wrl_ref, rbias_ref, tri_ref,
                  x1_ref, x1p_ref, eid_ref, wts_ref, rank_ref, cnt_ref,
                  xe_s, pa_s, z_s, ga_s, gb_s, ch_s, zsh_s, base_s,
                  *, ts, cw, nc, tiles_per_seq):
    i = pl.program_id(0)
    c = pl.program_id(1)
    pos = lax.rem(i, tiles_per_seq)

    @pl.when((i == 0) & (c == 0))
    def _():
        base_s[...] = jnp.zeros_like(base_s)

    @pl.when(c == 0)
    def _():
        keep_prev = (pos > 0).astype(_F32)
        keep_next = (pos < tiles_per_seq - 1).astype(_F32)
        xe_s[0:HALO, :] = (xp_ref[...] * keep_prev).astype(_BF16)
        xe_s[HALO:HALO + ts, :] = xm_ref[...].astype(_BF16)
        xe_s[HALO + ts:, :] = (xn_ref[...] * keep_next).astype(_BF16)

    xe = xe_s[...]
    xc = xe_s[HALO:HALO + ts, :]

    zsh_s[0] = _dot(xe, wv_ref[...]) * _sigmoid(_dot(xe, wg_ref[...]))
    span = zsh_s.shape[1] - V7X_SUBLANES
    for r in range(1, V7X_SUBLANES):
        zsh_s[r, 0:span, :] = zsh_s[0, pl.ds(r, span), :]

    ch_s[...] = _dot(xe, wh_ref[...]) * _dot(xe, wc_ref[...])
    u_b = _dot(xc, wb_ref[...])
    ga_s[c] = _sigmoid(_dot(xc, wga_ref[...]))
    gb_s[c] = _sigmoid(_dot(xc, wgb_ref[...]))

    conv_b = cbb_ref[...]
    for j in range(K_CONF):
        first = HALO - K_CONF // 2 + j
        aligned = first // V7X_SUBLANES * V7X_SUBLANES
        conv_b = conv_b + cbw_ref[j:j + 1, :] * zsh_s[first - aligned, pl.ds(aligned, ts), :]
    z_s[c] = conv_b

    conv_a = cab_ref[...]
    for j in range(K_SHORT):
        conv_a = conv_a + caw_ref[j:j + 1, :] * ch_s[pl.ds(HALO - K_SHORT // 2 + j, ts), :]
    pa_s[c] = (u_b * conv_a).astype(_BF16)

    @pl.when(c == nc - 1)
    def _():
        d_model = nc * cw
        mu = sum(jnp.sum(z_s[k], axis=-1, keepdims=True) for k in range(nc)) / d_model
        var = sum(jnp.sum((z_s[k] - mu) ** 2, axis=-1, keepdims=True) for k in range(nc)) / d_model
        inv = lax.rsqrt(var + LN_EPS)
        y_a = jnp.zeros((ts, d_model), _F32)
        y_b = jnp.zeros((ts, d_model), _F32)
        for k in range(nc):
            cols = slice(k * cw, (k + 1) * cw)
            zn = (z_s[k] - mu) * inv * lncg_ref[:, cols] + lncb_ref[:, cols]
            sw = zn * _sigmoid(zn)
            y_b = y_b + _dot(sw.astype(_BF16), wob_ref[cols, :])
            y_a = y_a + _dot(pa_s[k], woa_ref[cols, :])
        mix = jnp.zeros((ts, d_model), _F32)
        for k in range(nc):
            cols = slice(k * cw, (k + 1) * cw)
            merged = ga_s[k] * y_a[:, cols] + gb_s[k] * y_b[:, cols]
            mix = mix + _dot(merged.astype(_BF16), wo_ref[cols, :])
        x1 = _layer_norm_rows(ALPHA * xm_ref[...] + mix, ln1g_ref[...], ln1b_ref[...])
        x1_ref[...] = x1
        x1p_ref[...] = _pack_bf16_pairs(x1)

        xh = x1.astype(_BF16)
        xl = (x1 - xh.astype(_F32)).astype(_BF16)
        logits_t = _dot_nt(wrh_ref[...], xh) + (_dot_nt(wrl_ref[...], xh) + _dot_nt(wrh_ref[...], xl))
        ids, wts, onehot = _route(logits_t, rbias_ref[...], ts)

        before = _dot(onehot.astype(_BF16), tri_ref[...]) + base_s[:, 0:1]
        iota_e = lax.broadcasted_iota(jnp.int32, (N_EXPERTS, ts), 0)
        for k in range(TOP_K):
            eid_ref[k:k + 1, :] = ids[k]
            wts_ref[k:k + 1, :] = wts[k]
            r = jnp.sum(jnp.where(iota_e == ids[k], before, 0.0), axis=0, keepdims=True)
            rank_ref[k:k + 1, :] = r.astype(jnp.int32)
        base_s[...] = base_s[...] + jnp.sum(onehot, axis=1, keepdims=True)
        cnt_ref[...] = base_s[...].astype(jnp.int32)


def _mixer_call(x2d, seq_len, w_in_bf, conv_a_w, conv_a_b, conv_b_w, conv_b_b, ln_c_g, ln_c_b,
                w_out_a_bf, w_out_b_bf, w_o_bf, ln1_g, ln1_b, wr_hi, wr_lo, rbias, tri):
    t, d = x2d.shape
    ts, cw = SEQ_TILE, COL_CHUNK
    nc = d // cw
    assert seq_len % ts == 0 and d % cw == 0 and ts % HALO == 0
    n_tiles = t // ts
    halo_blocks_per_tile = ts // HALO
    n_halo_blocks = t // HALO

    def part_spec(p):
        return pl.BlockSpec((d, cw), lambda i, c, p=p: (0, p * nc + c))

    def chunk_rows(rows):
        return pl.BlockSpec((rows, cw), lambda i, c: (0, c))

    def const(shape):
        return pl.BlockSpec(shape, lambda i, c: (0,) * len(shape))

    tok_rows = lambda rows: pl.BlockSpec((rows, ts), lambda i, c: (0, i))

    in_specs = [
        pl.BlockSpec((ts, d), lambda i, c: (i, 0)),
        pl.BlockSpec((HALO, d), lambda i, c: (jnp.maximum(i * halo_blocks_per_tile - 1, 0), 0)),
        pl.BlockSpec((HALO, d),
                     lambda i, c: (jnp.minimum((i + 1) * halo_blocks_per_tile, n_halo_blocks - 1), 0)),
    ] + [part_spec(p) for p in range(N_IN_PARTS)] + [
        chunk_rows(K_SHORT), chunk_rows(1), chunk_rows(K_CONF), chunk_rows(1),
        const((1, d)), const((1, d)),
        const((d, d)), const((d, d)), const((d, d)),
        const((1, d)), const((1, d)),
        const((N_EXPERTS, d)), const((N_EXPERTS, d)), const((N_EXPERTS, 1)),
        const((ts, ts)),
    ]
    out_shape = [
        jax.ShapeDtypeStruct((t, d), _F32),
        jax.ShapeDtypeStruct((t, d // 2), jnp.int32),
        jax.ShapeDtypeStruct((TOP_K, t), jnp.int32),
        jax.ShapeDtypeStruct((TOP_K, t), _F32),
        jax.ShapeDtypeStruct((TOP_K, t), jnp.int32),
        jax.ShapeDtypeStruct((N_EXPERTS, V7X_LANES), jnp.int32),
    ]
    out_specs = [
        pl.BlockSpec((ts, d), lambda i, c: (i, 0)),
        pl.BlockSpec((ts, d // 2), lambda i, c: (i, 0)),
        tok_rows(TOP_K), tok_rows(TOP_K), tok_rows(TOP_K),
        const((N_EXPERTS, V7X_LANES)),
    ]
    scratch = [
        pltpu.VMEM((ts + 2 * HALO, d), _BF16),
        pltpu.VMEM((nc, ts, cw), _BF16),
        pltpu.VMEM((nc, ts, cw), _F32),
        pltpu.VMEM((nc, ts, cw), _F32),
        pltpu.VMEM((nc, ts, cw), _F32),
        pltpu.VMEM((ts + 2 * HALO, cw), _F32),
        pltpu.VMEM((V7X_SUBLANES, ts + 2 * HALO, cw), _F32),
        pltpu.VMEM((N_EXPERTS, V7X_LANES), _F32),
    ]
    kern = functools.partial(_mixer_kernel, ts=ts, cw=cw, nc=nc, tiles_per_seq=seq_len // ts)
    w_parts = [w_in_bf] * N_IN_PARTS
    return pl.pallas_call(
        kern,
        grid=(n_tiles, nc),
        in_specs=in_specs,
        out_specs=out_specs,
        out_shape=out_shape,
        scratch_shapes=scratch,
        compiler_params=pltpu.CompilerParams(
            dimension_semantics=("arbitrary", "arbitrary"),
            vmem_limit_bytes=V7X_VMEM_LIMIT_BYTES),
        name="mixer_router",
    )(x2d, x2d, x2d, *w_parts, conv_a_w, conv_a_b, conv_b_w, conv_b_b, ln_c_g, ln_c_b,
      w_out_a_bf, w_out_b_bf, w_o_bf, ln1_g, ln1_b, wr_hi, wr_lo, rbias, tri)


def _slot_kernel(pstart_ref, eid_ref, rank_ref, slot_ref):
    eid = eid_ref[...]

    def body(e, acc):
        return jnp.where(eid == e, pstart_ref[e], acc)

    slot_ref[...] = lax.fori_loop(0, N_EXPERTS, body, jnp.zeros_like(eid), unroll=8) + rank_ref[...]


def _slot_call(pstart, eid, rank):
    t = eid.shape[1]
    tl = min(SLOT_TILE, t)
    spec = pl.BlockSpec((TOP_K, tl), lambda i, *_: (0, i))
    return pl.pallas_call(
        _slot_kernel,
        grid_spec=pltpu.PrefetchScalarGridSpec(
            num_scalar_prefetch=1, grid=(t // tl,), in_specs=[spec, spec], out_specs=spec),
        out_shape=jax.ShapeDtypeStruct(eid.shape, jnp.int32),
        compiler_params=pltpu.CompilerParams(dimension_semantics=("arbitrary",)),
        name="slot_lookup",
    )(pstart, eid, rank)


def _sc_gather_rows(table, idx):
    n, = idx.shape
    d = table.shape[1]
    n_workers = V7X_SC_CORES * V7X_SC_SUBCORES
    ch = SC_CHUNK_ROWS
    per_worker = n // n_workers
    n_chunks = per_worker // ch
    assert n % (n_workers * ch) == 0 and n_chunks % 2 == 0
    mesh = plsc.VectorSubcoreMesh(core_axis_name="c", subcore_axis_name="s")

    @functools.partial(
        pl.kernel, mesh=mesh,
        out_type=jax.ShapeDtypeStruct((n, d), table.dtype),
        scratch_types=[pltpu.VMEM((ch,), jnp.int32), pltpu.VMEM((ch,), jnp.int32),
                       pltpu.VMEM((ch, d), table.dtype), pltpu.VMEM((ch, d), table.dtype),
                       pltpu.SemaphoreType.DMA, pltpu.SemaphoreType.DMA],
        name="sc_gather_rows",
    )
    def gather(table_hbm, idx_hbm, out_hbm, idx_v0, idx_v1, rows_v0, rows_v1, sem0, sem1):
        worker = lax.axis_index("s") * V7X_SC_CORES + lax.axis_index("c")
        base = worker * per_worker
        bufs = ((idx_v0, rows_v0, sem0), (idx_v1, rows_v1, sem1))

        def fetch(c, buf):
            idx_v, rows_v, sem = buf
            pltpu.sync_copy(idx_hbm.at[pl.ds(base + c * ch, ch)], idx_v)
            pltpu.make_async_copy(table_hbm.at[idx_v], rows_v, sem).start()

        def drain(c, buf):
            idx_v, rows_v, sem = buf
            pltpu.make_async_copy(table_hbm.at[idx_v], rows_v, sem).wait()
            pltpu.sync_copy(rows_v, out_hbm.at[pl.ds(base + c * ch, ch)])

        fetch(0, bufs[0])

        @pl.loop(0, n_chunks, step=2)
        def _(c):
            fetch(c + 1, bufs[1])
            drain(c, bufs[0])

            @pl.when(c + 2 < n_chunks)
            def _():
                fetch(c + 2, bufs[0])

            drain(c + 1, bufs[1])

    return gather(table, idx)


def _sc_dispatch_rows(x, slot_flat, n_rows):
    t, d = x.shape
    n_workers = V7X_SC_CORES * V7X_SC_SUBCORES
    ch = SC_CHUNK_ROWS
    per_worker = t // n_workers
    assert t % (n_workers * ch) == 0 and slot_flat.shape == (TOP_K * t,)
    mesh = plsc.VectorSubcoreMesh(core_axis_name="c", subcore_axis_name="s")

    @functools.partial(
        pl.kernel, mesh=mesh,
        out_type=jax.ShapeDtypeStruct((n_rows, d), x.dtype),
        scratch_types=[pltpu.VMEM((ch,), jnp.int32)] * TOP_K + [
            pltpu.VMEM((ch, d), x.dtype), pltpu.SemaphoreType.DMA, pltpu.SemaphoreType.DMA],
        name="sc_dispatch_rows",
    )
    def dispatch(x_hbm, slot_hbm, out_hbm, *scratch):
        idx_vs, rows_v, idx_sem, row_sem = scratch[:TOP_K], scratch[TOP_K], scratch[TOP_K + 1], scratch[TOP_K + 2]
        worker = lax.axis_index("s") * V7X_SC_CORES + lax.axis_index("c")
        base = worker * per_worker

        @pl.loop(0, per_worker // ch)
        def _(c):
            off = base + c * ch
            idx_copies = [pltpu.async_copy(slot_hbm.at[pl.ds(k * t + off, ch)], idx_vs[k], idx_sem)
                          for k in range(TOP_K)]
            pltpu.sync_copy(x_hbm.at[pl.ds(off, ch)], rows_v)
            for cp in idx_copies:
                cp.wait()
            row_copies = [pltpu.async_copy(rows_v, out_hbm.at[idx_vs[k]], row_sem) for k in range(TOP_K)]
            for cp in row_copies:
                cp.wait()

    return dispatch(x, slot_flat)


def _experts_kernel(blk_e_ref, blk_valid_ref, first_ref, next_e_ref, n_used_ref,
                    xs_hbm, wg_hbm, wu_hbm, wd_hbm, ys_hbm,
                    xbuf, ybuf, wg_f, wu_f, wd_f, wg_b, wu_b, wd_b, in_sem, out_sem, w_sem, *, bm):
    n_used = n_used_ref[0]
    half = xbuf.shape[2]

    def x_copy(b, slot):
        return pltpu.make_async_copy(xs_hbm.at[pl.ds(b * bm, bm)], xbuf.at[slot], in_sem.at[slot])

    def y_copy(b, slot):
        return pltpu.make_async_copy(ybuf.at[slot], ys_hbm.at[pl.ds(b * bm, bm)], out_sem.at[slot])

    def w_copies(e, wslot):
        return [pltpu.make_async_copy(src.at[e], dst.at[wslot], w_sem.at[wslot])
                for src, dst in ((wg_hbm, wg_f), (wu_hbm, wu_f), (wd_hbm, wd_f))]

    n_in, n_out = xbuf.shape[0], ybuf.shape[0]
    for j in range(n_in - 1):
        @pl.when(j < n_used)
        def _():
            x_copy(j, j).start()
    for cp in w_copies(blk_e_ref[0], 0):
        cp.start()

    def block(b, w_next):
        slot = lax.rem(b, n_out)
        in_slot = lax.rem(b, n_in)
        ahead = b + (n_in - 1)

        @pl.when(ahead < n_used)
        def _():
            x_copy(ahead, lax.rem(ahead, n_in)).start()

        is_first = first_ref[b] == 1

        @pl.when(is_first)
        def _():
            for cp in w_copies(blk_e_ref[b], w_next):
                cp.wait()
            wg_b[...] = wg_f[w_next].astype(_BF16)
            wu_b[...] = wu_f[w_next].astype(_BF16)
            wd_b[...] = wd_f[w_next].astype(_BF16)

            @pl.when(next_e_ref[b] >= 0)
            def _():
                for cp in w_copies(next_e_ref[b], 1 - w_next):
                    cp.start()

        x_copy(b, in_slot).wait()

        @pl.when(b >= n_out)
        def _():
            y_copy(b - n_out, slot).wait()

        row = lax.broadcasted_iota(jnp.int32, (bm, half), 0)
        x_hi, x_lo = _unpack_bf16_pairs(jnp.where(row < blk_valid_ref[b], xbuf[in_slot], 0))
        x_hi, x_lo = x_hi.astype(_BF16), x_lo.astype(_BF16)
        g = _dot(x_hi, wg_b[:half, :]) + _dot(x_lo, wg_b[half:, :])
        u = _dot(x_hi, wu_b[:half, :]) + _dot(x_lo, wu_b[half:, :])
        h = (g * _sigmoid(g) * u).astype(_BF16)
        ybuf[slot] = _pack_bf16_pairs(_dot(h, wd_b[...]))
        y_copy(b, slot).start()
        return jnp.where(is_first, 1 - w_next, w_next)

    lax.fori_loop(0, n_used, block, jnp.int32(0))

    for j in range(n_out):
        @pl.when(n_used > j)
        def _():
            last = n_used - 1 - j
            y_copy(last, lax.rem(last, n_out)).wait()


def _expert_rows(t):
    mean_rows = t * TOP_K // N_EXPERTS
    return 2 * EXPERT_ROWS if mean_rows >= 8 * EXPERT_ROWS else EXPERT_ROWS


def _experts_call(blk_e, blk_valid, blk_first, blk_next_e, n_used, xs, w_gate, w_up, w_down, bm):
    n_rows, dp = xs.shape
    d, f = w_gate.shape[-2:]
    assert d == 2 * dp
    any_spec = pl.BlockSpec(memory_space=pl.ANY)
    grid_spec = pltpu.PrefetchScalarGridSpec(
        num_scalar_prefetch=5,
        grid=(1,),
        in_specs=[any_spec] * 4,
        out_specs=any_spec,
        scratch_shapes=[
            pltpu.VMEM((EXPERT_IN_BUFFERS, bm, dp), jnp.int32),
            pltpu.VMEM((EXPERT_OUT_BUFFERS, bm, dp), jnp.int32),
            pltpu.VMEM((2, d, f), _F32), pltpu.VMEM((2, d, f), _F32), pltpu.VMEM((2, f, d), _F32),
            pltpu.VMEM((d, f), _BF16), pltpu.VMEM((d, f), _BF16), pltpu.VMEM((f, d), _BF16),
            pltpu.SemaphoreType.DMA((EXPERT_IN_BUFFERS,)), pltpu.SemaphoreType.DMA((EXPERT_OUT_BUFFERS,)),
            pltpu.SemaphoreType.DMA((2,)),
        ],
    )
    return pl.pallas_call(
        functools.partial(_experts_kernel, bm=bm),
        grid_spec=grid_spec,
        out_shape=jax.ShapeDtypeStruct((n_rows, dp), jnp.int32),
        compiler_params=pltpu.CompilerParams(
            dimension_semantics=("arbitrary",),
            vmem_limit_bytes=V7X_VMEM_LIMIT_BYTES),
        name="expert_ffn",
    )(blk_e, blk_valid, blk_first, blk_next_e, n_used, xs, w_gate, w_up, w_down)


def _combine_kernel(wts_ref, x1_ref, rows_ref, wsg_ref, wsu_ref, wsd_ref, g_ref, b_ref, out_ref):
    x1 = x1_ref[...]
    xb = x1.astype(_BF16)
    g = _dot(xb, wsg_ref[...])
    u = _dot(xb, wsu_ref[...])
    acc = ALPHA * x1 + _dot((g * _sigmoid(g) * u).astype(_BF16), wsd_ref[...])
    half = rows_ref.shape[2]
    acc_hi, acc_lo = acc[:, :half], acc[:, half:]
    w = wts_ref[...]
    for k in range(TOP_K):
        r_hi, r_lo = _unpack_bf16_pairs(rows_ref[k])
        acc_hi = acc_hi + r_hi * w[:, k:k + 1]
        acc_lo = acc_lo + r_lo * w[:, k:k + 1]
    acc = jnp.concatenate([acc_hi, acc_lo], axis=1)
    out_ref[...] = _layer_norm_rows(acc, g_ref[...], b_ref[...])


def _combine_call(wts_tk, x1, rows, ws_gate_bf, ws_up_bf, ws_down_bf, ln2_g, ln2_b):
    t, d = x1.shape
    tt = TOKEN_TILE
    fs = ws_gate_bf.shape[-1]

    def const(shape):
        return pl.BlockSpec(shape, lambda i: (0,) * len(shape))

    return pl.pallas_call(
        _combine_kernel,
        grid=(t // tt,),
        in_specs=[
            pl.BlockSpec((tt, TOP_K), lambda i: (i, 0)),
            pl.BlockSpec((tt, d), lambda i: (i, 0)),
            pl.BlockSpec((TOP_K, tt, d // 2), lambda i: (0, i, 0)),
            const((d, fs)), const((d, fs)), const((fs, d)),
            const((1, d)), const((1, d)),
        ],
        out_specs=pl.BlockSpec((tt, d), lambda i: (i, 0)),
        out_shape=jax.ShapeDtypeStruct((t, d), _F32),
        compiler_params=pltpu.CompilerParams(
            dimension_semantics=("arbitrary",),
            vmem_limit_bytes=V7X_VMEM_LIMIT_BYTES),
        name="combine_shared_norm",
    )(wts_tk, x1, rows, ws_gate_bf, ws_up_bf, ws_down_bf, ln2_g, ln2_b)


def _trunk(x, p):
    bsz, seq, d = x.shape
    t = bsz * seq
    x1, x1p, eid, wts, rank, cnt = _mixer_call(
        x.reshape(t, d), seq, p["w_in"], p["conv_a_w"], p["conv_a_b"], p["conv_b_w"], p["conv_b_b"],
        p["ln_c_g"], p["ln_c_b"], p["w_out_a"], p["w_out_b"], p["w_o"], p["ln1_g"], p["ln1_b"],
        p["wr_hi"], p["wr_lo"], p["rbias"], p["tri"])

    bm = _expert_rows(t)
    counts = cnt[:, 0]
    pcounts = (counts + bm - 1) // bm * bm
    pend = jnp.cumsum(pcounts)
    pstart = pend - pcounts
    n_blocks = (t * TOP_K + N_EXPERTS * (bm - 1)) // bm
    n_used = jnp.maximum(pend[-1] // bm, 1).astype(jnp.int32)
    blk_start = jnp.minimum(jnp.arange(n_blocks, dtype=jnp.int32), n_used - 1) * bm
    blk_e = jnp.minimum(jnp.sum(pend[None, :] <= blk_start[:, None], axis=1), N_EXPERTS - 1).astype(jnp.int32)
    blk_valid = jnp.clip(counts[blk_e] - (blk_start - pstart[blk_e]), 0, bm).astype(jnp.int32)
    blk_first = (blk_start == pstart[blk_e]).astype(jnp.int32)
    owner = jnp.where(counts > 0, jnp.arange(N_EXPERTS, dtype=jnp.int32), N_EXPERTS)
    next_owner = lax.cummin(jnp.concatenate([owner[1:], jnp.full((1,), N_EXPERTS, jnp.int32)]), reverse=True)
    blk_next_e = jnp.where(next_owner < N_EXPERTS, next_owner, -1)[blk_e].astype(jnp.int32)
    pstart = pstart.astype(jnp.int32)

    slot = _slot_call(pstart, eid, rank).reshape(TOP_K * t)
    xs = _sc_dispatch_rows(x1p, slot, n_blocks * bm)
    ys = _experts_call(blk_e, blk_valid, blk_first, blk_next_e, n_used.reshape(1), xs,
                       p["w_gate"], p["w_up"], p["w_down"], bm)
    rows = _sc_gather_rows(ys, slot).reshape(TOP_K, t, d // 2)
    out = _combine_call(wts.T, x1, rows, p["ws_gate"], p["ws_up"], p["ws_down"], p["ln2_g"], p["ln2_b"])
    return out.reshape(bsz, seq, d)


def _prepare_params(w_in, conv_a_w, conv_a_b, w_out_a, conv_b_w, conv_b_b, ln_c_g, ln_c_b,
                    w_out_b, w_o, ln1_g, ln1_b, w_router, router_bias, w_gate, w_up, w_down,
                    ws_gate, ws_up, ws_down, ln2_g, ln2_b):
    assert w_in.shape[0] == DEPTH == 1
    d = w_in.shape[1]
    wr = w_router[0].T
    wr_hi = wr.astype(_BF16)
    ts = SEQ_TILE
    return dict(
        w_in=w_in[0].astype(_BF16),
        conv_a_w=conv_a_w[0], conv_a_b=conv_a_b[0].reshape(1, d),
        conv_b_w=conv_b_w[0], conv_b_b=conv_b_b[0].reshape(1, d),
        ln_c_g=ln_c_g[0].reshape(1, d), ln_c_b=ln_c_b[0].reshape(1, d),
        w_out_a=w_out_a[0].astype(_BF16), w_out_b=w_out_b[0].astype(_BF16), w_o=w_o[0].astype(_BF16),
        ln1_g=ln1_g[0].reshape(1, d), ln1_b=ln1_b[0].reshape(1, d),
        wr_hi=wr_hi, wr_lo=(wr - wr_hi.astype(_F32)).astype(_BF16),
        rbias=router_bias[0].reshape(N_EXPERTS, 1),
        tri=(jnp.arange(ts)[:, None] < jnp.arange(ts)[None, :]).astype(_BF16),
        w_gate=w_gate[0], w_up=w_up[0], w_down=w_down[0],
        ws_gate=ws_gate[0].astype(_BF16), ws_up=ws_up[0].astype(_BF16), ws_down=ws_down[0].astype(_BF16),
        ln2_g=ln2_g[0].reshape(1, d), ln2_b=ln2_b[0].reshape(1, d),
    )


def kernel(x_prompt, x_sample, w_in, conv_a_w, conv_a_b, w_out_a, conv_b_w, conv_b_b, ln_c_g, ln_c_b,
           w_out_b, w_o, ln1_g, ln1_b, w_router, router_bias, w_gate, w_up, w_down, ws_gate, ws_up,
           ws_down, ln2_g, ln2_b):
    p = _prepare_params(w_in, conv_a_w, conv_a_b, w_out_a, conv_b_w, conv_b_b, ln_c_g, ln_c_b,
                        w_out_b, w_o, ln1_g, ln1_b, w_router, router_bias, w_gate, w_up, w_down,
                        ws_gate, ws_up, ws_down, ln2_g, ln2_b)
    return (_trunk(x_prompt, p), _trunk(x_sample, p))
```

```python
import functools

import jax
import jax.numpy as jnp
from jax import lax
from jax.experimental import pallas as pl
from jax.experimental.pallas import tpu as pltpu
from jax.experimental.pallas import tpu_sc as plsc

K_SHORT = 3
K_CONF = 31
N_EXPERTS = 256
TOP_K = 8
N_GROUPS = 8
TOPK_GROUPS = 4
GROUP_SIZE = N_EXPERTS // N_GROUPS
ROUTE_SCALE = 2.5
LN_EPS = 1e-5
DEPTH = 1
ALPHA = (2.0 * DEPTH) ** 0.25
N_IN_PARTS = 7

V7X_LANES = 128
V7X_SUBLANES = 8
V7X_VMEM_LIMIT_BYTES = 56 * 1024 * 1024

HALO = 16
SEQ_TILE = 512
COL_CHUNK = 256
EXPERT_ROWS = 256
EXPERT_IN_BUFFERS = 6
EXPERT_OUT_BUFFERS = 3
TOKEN_TILE = 512
COMBINE_PART_TOKENS = 16384
SLOT_TILE = 2048

V7X_SC_CORES = 2
V7X_SC_SUBCORES = 16
SC_CHUNK_ROWS = 64

_F32 = jnp.float32
_BF16 = jnp.bfloat16
_HIGH_HALF = -65536


def _dot(a, b):
    return jnp.dot(a, b, preferred_element_type=_F32)


def _dot_nt(a, b):
    return lax.dot_general(a, b, (((1,), (1,)), ((), ())), preferred_element_type=_F32)


def _sigmoid(v):
    return 1.0 / (1.0 + jnp.exp(-v))


def _pack_bf16_pairs(v):
    h = v.shape[1] // 2
    hi = lax.bitcast_convert_type(v[:, :h].astype(_BF16).astype(_F32), jnp.int32)
    lo = lax.bitcast_convert_type(v[:, h:].astype(_BF16).astype(_F32), jnp.int32)
    return (hi & _HIGH_HALF) | lax.shift_right_logical(lo, 16)


def _unpack_bf16_pairs(p):
    hi = lax.bitcast_convert_type(p & _HIGH_HALF, _F32)
    lo = lax.bitcast_convert_type(lax.shift_left(p, 16), _F32)
    return hi, lo


def _layer_norm_rows(v, g, b):
    mu = jnp.mean(v, axis=-1, keepdims=True)
    d = v - mu
    var = jnp.mean(d * d, axis=-1, keepdims=True)
    return d * lax.rsqrt(var + LN_EPS) * g + b


def _route(logits_t, rbias, ts):
    scores = _sigmoid(logits_t)
    biased = scores + rbias
    neg = -jnp.inf
    iota_g = lax.broadcasted_iota(jnp.int32, (GROUP_SIZE, ts), 0)
    gscore = []
    for g in range(N_GROUPS):
        v = biased[g * GROUP_SIZE:(g + 1) * GROUP_SIZE]
        m1 = jnp.max(v, axis=0, keepdims=True)
        i1 = jnp.min(jnp.where(v == m1, iota_g, GROUP_SIZE), axis=0, keepdims=True)
        m2 = jnp.max(jnp.where(iota_g == i1, neg, v), axis=0, keepdims=True)
        gscore.append(m1 + m2)
    masked = []
    for g in range(N_GROUPS):
        ahead = jnp.zeros((1, ts), jnp.int32)
        for h in range(N_GROUPS):
            if h == g:
                continue
            before = gscore[h] > gscore[g]
            if h < g:
                before = before | (gscore[h] == gscore[g])
            ahead = ahead + before.astype(jnp.int32)
        keep = ahead < TOPK_GROUPS
        masked.append(jnp.where(keep, biased[g * GROUP_SIZE:(g + 1) * GROUP_SIZE], neg))
    v = jnp.concatenate(masked, axis=0)

    iota_e = lax.broadcasted_iota(jnp.int32, (N_EXPERTS, ts), 0)
    ids, wts = [], []
    onehot = jnp.zeros((N_EXPERTS, ts), _F32)
    for _ in range(TOP_K):
        m = jnp.max(v, axis=0, keepdims=True)
        idx = jnp.min(jnp.where(v == m, iota_e, N_EXPERTS), axis=0, keepdims=True)
        hit = iota_e == idx
        wts.append(jnp.sum(jnp.where(hit, scores, 0.0), axis=0, keepdims=True))
        ids.append(idx)
        v = jnp.where(hit, neg, v)
        onehot = jnp.where(hit, 1.0, onehot)
    total = wts[0]
    for w in wts[1:]:
        total = total + w
    wts = [w / total * ROUTE_SCALE for w in wts]
    return ids, wts, onehot


def _mixer_kernel(xm_ref, xp_ref, xn_ref,
                  wh_ref, wb_ref, wc_ref, wv_ref, wg_ref, wga_ref, wgb_ref,
                  caw_ref, cab_ref, cbw_ref, cbb_ref, lncg_ref, lncb_ref,
                  woa_ref, wob_ref, wo_ref, ln1g_ref, ln1b_ref,
                  wrh_ref, wrl_ref, rbias_ref, tri_ref, after_ref,
                  x1_ref, x1p_ref, eid_ref, wts_ref, rank_ref, cnt_ref,
                  xe_s, pa_s, z_s, ga_s, gb_s, ch_s, zsh_s, base_s,
                  *, ts, cw, nc, tiles_per_seq):
    del after_ref
    i = pl.program_id(0)
    c = pl.program_id(1)
    pos = lax.rem(i, tiles_per_seq)

    @pl.when((i == 0) & (c == 0))
    def _():
        base_s[...] = jnp.zeros_like(base_s)

    @pl.when(c == 0)
    def _():
        keep_prev = (pos > 0).astype(_F32)
        keep_next = (pos < tiles_per_seq - 1).astype(_F32)
        xe_s[0:HALO, :] = (xp_ref[...] * keep_prev).astype(_BF16)
        xe_s[HALO:HALO + ts, :] = xm_ref[...].astype(_BF16)
        xe_s[HALO + ts:, :] = (xn_ref[...] * keep_next).astype(_BF16)

    xe = xe_s[...]
    xc = xe_s[HALO:HALO + ts, :]

    zsh_s[0] = _dot(xe, wv_ref[...]) * _sigmoid(_dot(xe, wg_ref[...]))
    span = zsh_s.shape[1] - V7X_SUBLANES
    for r in range(1, V7X_SUBLANES):
        zsh_s[r, 0:span, :] = zsh_s[0, pl.ds(r, span), :]

    ch_s[...] = _dot(xe, wh_ref[...]) * _dot(xe, wc_ref[...])
    u_b = _dot(xc, wb_ref[...])
    ga_s[c] = _sigmoid(_dot(xc, wga_ref[...]))
    gb_s[c] = _sigmoid(_dot(xc, wgb_ref[...]))

    conv_b = cbb_ref[...]
    for j in range(K_CONF):
        first = HALO - K_CONF // 2 + j
        aligned = first // V7X_SUBLANES * V7X_SUBLANES
        conv_b = conv_b + cbw_ref[j:j + 1, :] * zsh_s[first - aligned, pl.ds(aligned, ts), :]
    z_s[c] = conv_b

    conv_a = cab_ref[...]
    for j in range(K_SHORT):
        conv_a = conv_a + caw_ref[j:j + 1, :] * ch_s[pl.ds(HALO - K_SHORT // 2 + j, ts), :]
    pa_s[c] = (u_b * conv_a).astype(_BF16)

    @pl.when(c == nc - 1)
    def _():
        d_model = nc * cw
        mu = sum(jnp.sum(z_s[k], axis=-1, keepdims=True) for k in range(nc)) / d_model
        var = sum(jnp.sum((z_s[k] - mu) ** 2, axis=-1, keepdims=True) for k in range(nc)) / d_model
        inv = lax.rsqrt(var + LN_EPS)
        y_a = jnp.zeros((ts, d_model), _F32)
        y_b = jnp.zeros((ts, d_model), _F32)
        for k in range(nc):
            cols = slice(k * cw, (k + 1) * cw)
            zn = (z_s[k] - mu) * inv * lncg_ref[:, cols] + lncb_ref[:, cols]
            sw = zn * _sigmoid(zn)
            y_b = y_b + _dot(sw.astype(_BF16), wob_ref[cols, :])
            y_a = y_a + _dot(pa_s[k], woa_ref[cols, :])
        mix = jnp.zeros((ts, d_model), _F32)
        for k in range(nc):
            cols = slice(k * cw, (k + 1) * cw)
            merged = ga_s[k] * y_a[:, cols] + gb_s[k] * y_b[:, cols]
            mix = mix + _dot(merged.astype(_BF16), wo_ref[cols, :])
        x1 = _layer_norm_rows(ALPHA * xm_ref[...] + mix, ln1g_ref[...], ln1b_ref[...])
        x1_ref[...] = x1
        x1p_ref[...] = _pack_bf16_pairs(x1)

        xh = x1.astype(_BF16)
        xl = (x1 - xh.astype(_F32)).astype(_BF16)
        logits_t = _dot_nt(wrh_ref[...], xh) + (_dot_nt(wrl_ref[...], xh) + _dot_nt(wrh_ref[...], xl))
        ids, wts, onehot = _route(logits_t, rbias_ref[...], ts)

        before = _dot(onehot.astype(_BF16), tri_ref[...]) + base_s[:, 0:1]
        iota_e = lax.broadcasted_iota(jnp.int32, (N_EXPERTS, ts), 0)
        for k in range(TOP_K):
            eid_ref[k:k + 1, :] = ids[k]
            wts_ref[k:k + 1, :] = wts[k]
            r = jnp.sum(jnp.where(iota_e == ids[k], before, 0.0), axis=0, keepdims=True)
            rank_ref[k:k + 1, :] = r.astype(jnp.int32)
        base_s[...] = base_s[...] + jnp.sum(onehot, axis=1, keepdims=True)
        cnt_ref[...] = base_s[...].astype(jnp.int32)


def _mixer_call(x2d, seq_len, w_in_bf, conv_a_w, conv_a_b, conv_b_w, conv_b_b, ln_c_g, ln_c_b,
                w_out_a_bf, w_out_b_bf, w_o_bf, ln1_g, ln1_b, wr_hi, wr_lo, rbias, tri, after):
    t, d = x2d.shape
    ts, cw = SEQ_TILE, COL_CHUNK
    nc = d // cw
    assert seq_len % ts == 0 and d % cw == 0 and ts % HALO == 0
    n_tiles = t // ts
    halo_blocks_per_tile = ts // HALO
    n_halo_blocks = t // HALO

    def part_spec(p):
        return pl.BlockSpec((d, cw), lambda i, c, p=p: (0, p * nc + c))

    def chunk_rows(rows):
        return pl.BlockSpec((rows, cw), lambda i, c: (0, c))

    def const(shape):
        return pl.BlockSpec(shape, lambda i, c: (0,) * len(shape))

    tok_rows = lambda rows: pl.BlockSpec((rows, ts), lambda i, c: (0, i))

    in_specs = [
        pl.BlockSpec((ts, d), lambda i, c: (i, 0)),
        pl.BlockSpec((HALO, d), lambda i, c: (jnp.maximum(i * halo_blocks_per_tile - 1, 0), 0)),
        pl.BlockSpec((HALO, d),
                     lambda i, c: (jnp.minimum((i + 1) * halo_blocks_per_tile, n_halo_blocks - 1), 0)),
    ] + [part_spec(p) for p in range(N_IN_PARTS)] + [
        chunk_rows(K_SHORT), chunk_rows(1), chunk_rows(K_CONF), chunk_rows(1),
        const((1, d)), const((1, d)),
        const((d, d)), const((d, d)), const((d, d)),
        const((1, d)), const((1, d)),
        const((N_EXPERTS, d)), const((N_EXPERTS, d)), const((N_EXPERTS, 1)),
        const((ts, ts)),
        pl.BlockSpec(memory_space=pl.ANY),
    ]
    out_shape = [
        jax.ShapeDtypeStruct((t, d), _F32),
        jax.ShapeDtypeStruct((t, d // 2), jnp.int32),
        jax.ShapeDtypeStruct((TOP_K, t), jnp.int32),
        jax.ShapeDtypeStruct((TOP_K, t), _F32),
        jax.ShapeDtypeStruct((TOP_K, t), jnp.int32),
        jax.ShapeDtypeStruct((N_EXPERTS, V7X_LANES), jnp.int32),
    ]
    out_specs = [
        pl.BlockSpec((ts, d), lambda i, c: (i, 0)),
        pl.BlockSpec((ts, d // 2), lambda i, c: (i, 0)),
        tok_rows(TOP_K), tok_rows(TOP_K), tok_rows(TOP_K),
        const((N_EXPERTS, V7X_LANES)),
    ]
    scratch = [
        pltpu.VMEM((ts + 2 * HALO, d), _BF16),
        pltpu.VMEM((nc, ts, cw), _BF16),
        pltpu.VMEM((nc, ts, cw), _F32),
        pltpu.VMEM((nc, ts, cw), _F32),
        pltpu.VMEM((nc, ts, cw), _F32),
        pltpu.VMEM((ts + 2 * HALO, cw), _F32),
        pltpu.VMEM((V7X_SUBLANES, ts + 2 * HALO, cw), _F32),
        pltpu.VMEM((N_EXPERTS, V7X_LANES), _F32),
    ]
    kern = functools.partial(_mixer_kernel, ts=ts, cw=cw, nc=nc, tiles_per_seq=seq_len // ts)
    w_parts = [w_in_bf] * N_IN_PARTS
    return pl.pallas_call(
        kern,
        grid=(n_tiles, nc),
        in_specs=in_specs,
        out_specs=out_specs,
        out_shape=out_shape,
        scratch_shapes=scratch,
        compiler_params=pltpu.CompilerParams(
            dimension_semantics=("arbitrary", "arbitrary"),
            vmem_limit_bytes=V7X_VMEM_LIMIT_BYTES),
        name="mixer_router",
    )(x2d, x2d, x2d, *w_parts, conv_a_w, conv_a_b, conv_b_w, conv_b_b, ln_c_g, ln_c_b,
      w_out_a_bf, w_out_b_bf, w_o_bf, ln1_g, ln1_b, wr_hi, wr_lo, rbias, tri, after)


def _slot_kernel(pstart_ref, eid_ref, rank_ref, slot_ref):
    eid = eid_ref[...]

    def body(e, acc):
        return jnp.where(eid == e, pstart_ref[e], acc)

    slot_ref[...] = lax.fori_loop(0, N_EXPERTS, body, jnp.zeros_like(eid), unroll=8) + rank_ref[...]


def _slot_call(pstart, eid, rank):
    t = eid.shape[1]
    tl = min(SLOT_TILE, t)
    spec = pl.BlockSpec((TOP_K, tl), lambda i, *_: (0, i))
    return pl.pallas_call(
        _slot_kernel,
        grid_spec=pltpu.PrefetchScalarGridSpec(
            num_scalar_prefetch=1, grid=(t // tl,), in_specs=[spec, spec], out_specs=spec),
        out_shape=jax.ShapeDtypeStruct(eid.shape, jnp.int32),
        compiler_params=pltpu.CompilerParams(dimension_semantics=("arbitrary",)),
        name="slot_lookup",
    )(pstart, eid, rank)


def _sc_gather_rows(table, idx):
    n, = idx.shape
    d = table.shape[1]
    n_workers = V7X_SC_CORES * V7X_SC_SUBCORES
    ch = SC_CHUNK_ROWS
    per_worker = n // n_workers
    n_chunks = per_worker // ch
    assert n % (n_workers * ch) == 0 and n_chunks % 2 == 0
    mesh = plsc.VectorSubcoreMesh(core_axis_name="c", subcore_axis_name="s")

    @functools.partial(
        pl.kernel, mesh=mesh,
        out_type=jax.ShapeDtypeStruct((n, d), table.dtype),
        scratch_types=[pltpu.VMEM((ch,), jnp.int32), pltpu.VMEM((ch,), jnp.int32),
                       pltpu.VMEM((ch, d), table.dtype), pltpu.VMEM((ch, d), table.dtype),
                       pltpu.SemaphoreType.DMA, pltpu.SemaphoreType.DMA],
        name="sc_gather_rows",
    )
    def gather(table_hbm, idx_hbm, out_hbm, idx_v0, idx_v1, rows_v0, rows_v1, sem0, sem1):
        worker = lax.axis_index("s") * V7X_SC_CORES + lax.axis_index("c")
        base = worker * per_worker
        bufs = ((idx_v0, rows_v0, sem0), (idx_v1, rows_v1, sem1))

        def fetch(c, buf):
            idx_v, rows_v, sem = buf
            pltpu.sync_copy(idx_hbm.at[pl.ds(base + c * ch, ch)], idx_v)
            pltpu.make_async_copy(table_hbm.at[idx_v], rows_v, sem).start()

        def drain(c, buf):
            idx_v, rows_v, sem = buf
            pltpu.make_async_copy(table_hbm.at[idx_v], rows_v, sem).wait()
            pltpu.sync_copy(rows_v, out_hbm.at[pl.ds(base + c * ch, ch)])

        fetch(0, bufs[0])

        @pl.loop(0, n_chunks, step=2)
        def _(c):
            fetch(c + 1, bufs[1])
            drain(c, bufs[0])

            @pl.when(c + 2 < n_chunks)
            def _():
                fetch(c + 2, bufs[0])

            drain(c + 1, bufs[1])

    return gather(table, idx)


def _sc_dispatch_rows(x, slot_flat, n_rows):
    t, d = x.shape
    n_workers = V7X_SC_CORES * V7X_SC_SUBCORES
    ch = SC_CHUNK_ROWS
    per_worker = t // n_workers
    assert t % (n_workers * ch) == 0 and slot_flat.shape == (TOP_K * t,)
    mesh = plsc.VectorSubcoreMesh(core_axis_name="c", subcore_axis_name="s")

    @functools.partial(
        pl.kernel, mesh=mesh,
        out_type=jax.ShapeDtypeStruct((n_rows, d), x.dtype),
        scratch_types=[pltpu.VMEM((ch,), jnp.int32)] * TOP_K + [
            pltpu.VMEM((ch, d), x.dtype), pltpu.SemaphoreType.DMA, pltpu.SemaphoreType.DMA],
        name="sc_dispatch_rows",
    )
    def dispatch(x_hbm, slot_hbm, out_hbm, *scratch):
        idx_vs, rows_v, idx_sem, row_sem = scratch[:TOP_K], scratch[TOP_K], scratch[TOP_K + 1], scratch[TOP_K + 2]
        worker = lax.axis_index("s") * V7X_SC_CORES + lax.axis_index("c")
        base = worker * per_worker

        @pl.loop(0, per_worker // ch)
        def _(c):
            off = base + c * ch
            idx_copies = [pltpu.async_copy(slot_hbm.at[pl.ds(k * t + off, ch)], idx_vs[k], idx_sem)
                          for k in range(TOP_K)]
            pltpu.sync_copy(x_hbm.at[pl.ds(off, ch)], rows_v)
            for cp in idx_copies:
                cp.wait()
            row_copies = [pltpu.async_copy(rows_v, out_hbm.at[idx_vs[k]], row_sem) for k in range(TOP_K)]
            for cp in row_copies:
                cp.wait()

    return dispatch(x, slot_flat)


def _experts_kernel(blk_e_ref, blk_valid_ref, first_ref, next_e_ref, n_used_ref,
                    xs_hbm, wg_hbm, wu_hbm, wd_hbm, after_hbm, ys_hbm,
                    xbuf, ybuf, wg_f, wu_f, wd_f, wg_b, wu_b, wd_b, in_sem, out_sem, w_sem, *, bm):
    del after_hbm
    n_used = n_used_ref[0]
    half = xbuf.shape[2]

    def x_copy(b, slot):
        return pltpu.make_async_copy(xs_hbm.at[pl.ds(b * bm, bm)], xbuf.at[slot], in_sem.at[slot])

    def y_copy(b, slot):
        return pltpu.make_async_copy(ybuf.at[slot], ys_hbm.at[pl.ds(b * bm, bm)], out_sem.at[slot])

    def w_copies(e, wslot):
        return [pltpu.make_async_copy(src.at[e], dst.at[wslot], w_sem.at[wslot])
                for src, dst in ((wg_hbm, wg_f), (wu_hbm, wu_f), (wd_hbm, wd_f))]

    n_in, n_out = xbuf.shape[0], ybuf.shape[0]
    for j in range(n_in - 1):
        @pl.when(j < n_used)
        def _():
            x_copy(j, j).start()
    for cp in w_copies(blk_e_ref[0], 0):
        cp.start()

    def block(b, w_next):
        slot = lax.rem(b, n_out)
        in_slot = lax.rem(b, n_in)
        ahead = b + (n_in - 1)

        @pl.when(ahead < n_used)
        def _():
            x_copy(ahead, lax.rem(ahead, n_in)).start()

        is_first = first_ref[b] == 1

        @pl.when(is_first)
        def _():
            for cp in w_copies(blk_e_ref[b], w_next):
                cp.wait()
            wg_b[...] = wg_f[w_next].astype(_BF16)
            wu_b[...] = wu_f[w_next].astype(_BF16)
            wd_b[...] = wd_f[w_next].astype(_BF16)

            @pl.when(next_e_ref[b] >= 0)
            def _():
                for cp in w_copies(next_e_ref[b], 1 - w_next):
                    cp.start()

        x_copy(b, in_slot).wait()

        @pl.when(b >= n_out)
        def _():
            y_copy(b - n_out, slot).wait()

        row = lax.broadcasted_iota(jnp.int32, (bm, half), 0)
        x_hi, x_lo = _unpack_bf16_pairs(jnp.where(row < blk_valid_ref[b], xbuf[in_slot], 0))
        x_hi, x_lo = x_hi.astype(_BF16), x_lo.astype(_BF16)
        g = _dot(x_hi, wg_b[:half, :]) + _dot(x_lo, wg_b[half:, :])
        u = _dot(x_hi, wu_b[:half, :]) + _dot(x_lo, wu_b[half:, :])
        h = (g * _sigmoid(g) * u).astype(_BF16)
        ybuf[slot] = _pack_bf16_pairs(_dot(h, wd_b[...]))
        y_copy(b, slot).start()
        return jnp.where(is_first, 1 - w_next, w_next)

    lax.fori_loop(0, n_used, block, jnp.int32(0))

    for j in range(n_out):
        @pl.when(n_used > j)
        def _():
            last = n_used - 1 - j
            y_copy(last, lax.rem(last, n_out)).wait()


def _expert_rows(t):
    mean_rows = t * TOP_K // N_EXPERTS
    return 2 * EXPERT_ROWS if mean_rows >= 8 * EXPERT_ROWS else EXPERT_ROWS


def _experts_call(blk_e, blk_valid, blk_first, blk_next_e, n_used, xs, w_gate, w_up, w_down, after, bm):
    n_rows, dp = xs.shape
    d, f = w_gate.shape[-2:]
    assert d == 2 * dp
    any_spec = pl.BlockSpec(memory_space=pl.ANY)
    grid_spec = pltpu.PrefetchScalarGridSpec(
        num_scalar_prefetch=5,
        grid=(1,),
        in_specs=[any_spec] * 5,
        out_specs=any_spec,
        scratch_shapes=[
            pltpu.VMEM((EXPERT_IN_BUFFERS, bm, dp), jnp.int32),
            pltpu.VMEM((EXPERT_OUT_BUFFERS, bm, dp), jnp.int32),
            pltpu.VMEM((2, d, f), _F32), pltpu.VMEM((2, d, f), _F32), pltpu.VMEM((2, f, d), _F32),
            pltpu.VMEM((d, f), _BF16), pltpu.VMEM((d, f), _BF16), pltpu.VMEM((f, d), _BF16),
            pltpu.SemaphoreType.DMA((EXPERT_IN_BUFFERS,)), pltpu.SemaphoreType.DMA((EXPERT_OUT_BUFFERS,)),
            pltpu.SemaphoreType.DMA((2,)),
        ],
    )
    return pl.pallas_call(
        functools.partial(_experts_kernel, bm=bm),
        grid_spec=grid_spec,
        out_shape=jax.ShapeDtypeStruct((n_rows, dp), jnp.int32),
        compiler_params=pltpu.CompilerParams(
            dimension_semantics=("arbitrary",),
            vmem_limit_bytes=V7X_VMEM_LIMIT_BYTES),
        name="expert_ffn",
    )(blk_e, blk_valid, blk_first, blk_next_e, n_used, xs, w_gate, w_up, w_down, after)


def _combine_kernel(wts_ref, x1_ref, rows_ref, wsg_ref, wsu_ref, wsd_ref, g_ref, b_ref, *rest):
    out_ref = rest[-1]
    x1 = x1_ref[...]
    xb = x1.astype(_BF16)
    g = _dot(xb, wsg_ref[...])
    u = _dot(xb, wsu_ref[...])
    acc = ALPHA * x1 + _dot((g * _sigmoid(g) * u).astype(_BF16), wsd_ref[...])
    half = rows_ref.shape[2]
    acc_hi, acc_lo = acc[:, :half], acc[:, half:]
    w = wts_ref[...]
    for k in range(TOP_K):
        r_hi, r_lo = _unpack_bf16_pairs(rows_ref[k])
        acc_hi = acc_hi + r_hi * w[:, k:k + 1]
        acc_lo = acc_lo + r_lo * w[:, k:k + 1]
    acc = jnp.concatenate([acc_hi, acc_lo], axis=1)
    out_ref[...] = _layer_norm_rows(acc, g_ref[...], b_ref[...])


def _combine_call(part, prev_out, wts_tk, x1, rows, ws_gate_bf, ws_up_bf, ws_down_bf, ln2_g, ln2_b):
    t, d = x1.shape
    tp = rows.shape[1]
    tt = TOKEN_TILE
    fs = ws_gate_bf.shape[-1]
    first = part * (tp // tt)

    def const(shape):
        return pl.BlockSpec(shape, lambda i: (0,) * len(shape))

    in_specs = [
        pl.BlockSpec((tt, TOP_K), lambda i: (first + i, 0)),
        pl.BlockSpec((tt, d), lambda i: (first + i, 0)),
        pl.BlockSpec((TOP_K, tt, d // 2), lambda i: (0, i, 0)),
        const((d, fs)), const((d, fs)), const((fs, d)),
        const((1, d)), const((1, d)),
    ]
    args = [wts_tk, x1, rows, ws_gate_bf, ws_up_bf, ws_down_bf, ln2_g, ln2_b]
    aliases = {}
    if prev_out is not None:
        in_specs.append(pl.BlockSpec(memory_space=pl.ANY))
        args.append(prev_out)
        aliases = {len(args) - 1: 0}
    return pl.pallas_call(
        _combine_kernel,
        grid=(tp // tt,),
        in_specs=in_specs,
        out_specs=pl.BlockSpec((tt, d), lambda i: (first + i, 0)),
        out_shape=jax.ShapeDtypeStruct((t, d), _F32),
        input_output_aliases=aliases,
        compiler_params=pltpu.CompilerParams(
            dimension_semantics=("arbitrary",),
            vmem_limit_bytes=V7X_VMEM_LIMIT_BYTES),
        name="combine_shared_norm",
    )(*args)


def _stage_mixer(x, p, after):
    bsz, seq, d = x.shape
    t = bsz * seq
    x1, x1p, eid, wts, rank, cnt = _mixer_call(
        x.reshape(t, d), seq, p["w_in"], p["conv_a_w"], p["conv_a_b"], p["conv_b_w"], p["conv_b_b"],
        p["ln_c_g"], p["ln_c_b"], p["w_out_a"], p["w_out_b"], p["w_o"], p["ln1_g"], p["ln1_b"],
        p["wr_hi"], p["wr_lo"], p["rbias"], p["tri"], after)

    bm = _expert_rows(t)
    counts = cnt[:, 0]
    pcounts = (counts + bm - 1) // bm * bm
    pend = jnp.cumsum(pcounts)
    pstart = pend - pcounts
    n_blocks = (t * TOP_K + N_EXPERTS * (bm - 1)) // bm
    n_used = jnp.maximum(pend[-1] // bm, 1).astype(jnp.int32)
    blk_start = jnp.minimum(jnp.arange(n_blocks, dtype=jnp.int32), n_used - 1) * bm
    blk_e = jnp.minimum(jnp.sum(pend[None, :] <= blk_start[:, None], axis=1), N_EXPERTS - 1).astype(jnp.int32)
    blk_valid = jnp.clip(counts[blk_e] - (blk_start - pstart[blk_e]), 0, bm).astype(jnp.int32)
    blk_first = (blk_start == pstart[blk_e]).astype(jnp.int32)
    owner = jnp.where(counts > 0, jnp.arange(N_EXPERTS, dtype=jnp.int32), N_EXPERTS)
    next_owner = lax.cummin(jnp.concatenate([owner[1:], jnp.full((1,), N_EXPERTS, jnp.int32)]), reverse=True)
    blk_next_e = jnp.where(next_owner < N_EXPERTS, next_owner, -1)[blk_e].astype(jnp.int32)

    slot = _slot_call(pstart.astype(jnp.int32), eid, rank)
    xs = _sc_dispatch_rows(x1p, slot.reshape(TOP_K * t), n_blocks * bm)
    return dict(shape=(bsz, seq, d), x1=x1, wts_tk=wts.T, cnt=cnt, slot=slot, xs=xs, bm=bm,
                blocks=(blk_e, blk_valid, blk_first, blk_next_e, n_used.reshape(1)))


def _stage_experts(st, p, after):
    return _experts_call(*st["blocks"], st["xs"], p["w_gate"], p["w_up"], p["w_down"], after, st["bm"])


def _stage_combine(st, ys, p):
    bsz, seq, d = st["shape"]
    t = bsz * seq
    tp = min(COMBINE_PART_TOKENS, t)
    out = None
    for part in range(t // tp):
        idx = st["slot"][:, part * tp:(part + 1) * tp].reshape(TOP_K * tp)
        rows = _sc_gather_rows(ys, idx).reshape(TOP_K, tp, d // 2)
        out = _combine_call(part, out, st["wts_tk"], st["x1"], rows, p["ws_gate"], p["ws_up"], p["ws_down"],
                            p["ln2_g"], p["ln2_b"])
    return out.reshape(bsz, seq, d)


def _trunk(x, p):
    st = _stage_mixer(x, p, p["rbias"])
    ys = _stage_experts(st, p, p["rbias"])
    return _stage_combine(st, ys, p)


def _prepare_params(w_in, conv_a_w, conv_a_b, w_out_a, conv_b_w, conv_b_b, ln_c_g, ln_c_b,
                    w_out_b, w_o, ln1_g, ln1_b, w_router, router_bias, w_gate, w_up, w_down,
                    ws_gate, ws_up, ws_down, ln2_g, ln2_b):
    assert w_in.shape[0] == DEPTH == 1
    d = w_in.shape[1]
    wr = w_router[0].T
    wr_hi = wr.astype(_BF16)
    ts = SEQ_TILE
    return dict(
        w_in=w_in[0].astype(_BF16),
        conv_a_w=conv_a_w[0], conv_a_b=conv_a_b[0].reshape(1, d),
        conv_b_w=conv_b_w[0], conv_b_b=conv_b_b[0].reshape(1, d),
        ln_c_g=ln_c_g[0].reshape(1, d), ln_c_b=ln_c_b[0].reshape(1, d),
        w_out_a=w_out_a[0].astype(_BF16), w_out_b=w_out_b[0].astype(_BF16), w_o=w_o[0].astype(_BF16),
        ln1_g=ln1_g[0].reshape(1, d), ln1_b=ln1_b[0].reshape(1, d),
        wr_hi=wr_hi, wr_lo=(wr - wr_hi.astype(_F32)).astype(_BF16),
        rbias=router_bias[0].reshape(N_EXPERTS, 1),
        tri=(jnp.arange(ts)[:, None] < jnp.arange(ts)[None, :]).astype(_BF16),
        w_gate=w_gate[0], w_up=w_up[0], w_down=w_down[0],
        ws_gate=ws_gate[0].astype(_BF16), ws_up=ws_up[0].astype(_BF16), ws_down=ws_down[0].astype(_BF16),
        ln2_g=ln2_g[0].reshape(1, d), ln2_b=ln2_b[0].reshape(1, d),
    )


def kernel(x_prompt, x_sample, w_in, conv_a_w, conv_a_b, w_out_a, conv_b_w, conv_b_b, ln_c_g, ln_c_b,
           w_out_b, w_o, ln1_g, ln1_b, w_router, router_bias, w_gate, w_up, w_down, ws_gate, ws_up,
           ws_down, ln2_g, ln2_b):
    p = _prepare_params(w_in, conv_a_w, conv_a_b, w_out_a, conv_b_w, conv_b_b, ln_c_g, ln_c_b,
                        w_out_b, w_o, ln1_g, ln1_b, w_router, router_bias, w_gate, w_up, w_down,
                        ws_gate, ws_up, ws_down, ln2_g, ln2_b)
    small = _stage_mixer(x_prompt, p, p["rbias"])
    large = _stage_mixer(x_sample, p, small["cnt"])
    ys_small = _stage_experts(small, p, p["rbias"])
    ys_large = _stage_experts(large, p, ys_small)
    y_sample = _stage_combine(large, ys_large, p)
    y_prompt = _stage_combine(small, ys_small, p)
    return (y_prompt, y_sample)
```
